```python
import jax, jax.numpy as jnp
from jax import lax
import numpy as np

D_MODEL = 1024
BATCH = 2
SEQ = 8192
DEPTH = 1

N_META = 16
NORM_EPS = 1e-6
D_MIX = D_MODEL
D_LRU = D_MIX // 2
D_RWKV = D_MIX - D_LRU
LRU_HEADS = 8
LRU_HEAD_DIM = D_LRU // LRU_HEADS
CONV_WIDTH = 4
LRU_C = 8.0
RWKV_HEAD_DIM = 64
RWKV_HEADS = D_RWKV // RWKV_HEAD_DIM
D_DECAY_LORA = 64
D_AAA_LORA = 64
D_GATE_LORA = 128
RWKV_GN_EPS = 64e-5
PEER_HEADS = 8
PEER_N_KEYS = 128
PEER_N_EXPERTS = PEER_N_KEYS * PEER_N_KEYS
PEER_D_QUERY = 256
PEER_HALF = PEER_D_QUERY // 2
PEER_TOPK = 16
PEER_BLOCK = 256
D_RWKV_IN = 3 * D_RWKV + D_DECAY_LORA + D_AAA_LORA + D_GATE_LORA
D_IN = 2 * D_LRU + D_RWKV_IN
RWKV_SPLITS = [D_RWKV, 2 * D_RWKV, 3 * D_RWKV, 3 * D_RWKV + D_DECAY_LORA,
               3 * D_RWKV + D_DECAY_LORA + D_AAA_LORA]

kernel_name = "hymba_rglru_rwkv7_peer_block"


def rmsnorm(x, gain):
    xf = x.astype(jnp.float32)
    y = xf * lax.rsqrt(jnp.mean(xf * xf, axis=-1, keepdims=True) + NORM_EPS)
    return (y * gain.astype(jnp.float32)).astype(x.dtype)


def causal_depthwise_conv(x, w, b):
    y = lax.conv_general_dilated(
        x, w[:, None, :].astype(x.dtype), window_strides=(1,),
        padding=[(CONV_WIDTH - 1, 0)], dimension_numbers=("NWC", "WIO", "NWC"),
        feature_group_count=x.shape[-1])
    return y + b


def rg_lru(x, w_ga, b_ga, w_gx, b_gx, lam):
    B, T, _ = x.shape
    xh = x.reshape(B, T, LRU_HEADS, LRU_HEAD_DIM)
    r = jax.nn.sigmoid(jnp.einsum("bthi,hij->bthj", xh, w_ga).reshape(B, T, D_LRU) + b_ga)
    i = jax.nn.sigmoid(jnp.einsum("bthi,hij->bthj", xh, w_gx).reshape(B, T, D_LRU) + b_gx)
    log_a = -LRU_C * r.astype(jnp.float32) * jax.nn.softplus(-lam.astype(jnp.float32))
    a = jnp.exp(log_a)
    b_in = jnp.sqrt(-jnp.expm1(2.0 * log_a)) * (i * x).astype(jnp.float32)

    def combine(c1, c2):
        a1, b1 = c1
        a2, b2 = c2
        return a1 * a2, a2 * b1 + b2

    _, h = lax.associative_scan(combine, (a, b_in), axis=1)
    return h.astype(x.dtype)


def token_shift(p, mu):
    prev = jnp.pad(p[:, :-1], ((0, 0), (1, 0), (0, 0)))
    return p + (prev - p) * mu


def rwkv7_time_mix(r, k, v, w_lr, a_lr, g_lr, w0, w2, a0, a2, g2, k_k, k_a, r_k, gn_w, gn_b):
    B, T, _ = r.shape
    H, N = RWKV_HEADS, RWKV_HEAD_DIM
    log_w = -jax.nn.softplus(-(w0 + jnp.tanh(w_lr) @ w2)) - 0.5
    decay = jnp.exp(-jnp.exp(log_w.astype(jnp.float32)))
    a = jax.nn.sigmoid(a0 + a_lr @ a2)
    g = jax.nn.sigmoid(g_lr) @ g2

    def heads(t):
        return t.reshape(B, T, H, N).astype(jnp.float32)

    kk = heads(k * k_k)
    kk = kk / jnp.maximum(jnp.sqrt(jnp.sum(kk * kk, axis=-1, keepdims=True)), 1e-12)
    k = k * (1.0 + (a - 1.0) * k_a)
    rh, kh, vh, ah = heads(r), heads(k), heads(v), heads(a)

    def step(S, inp):
        r_t, w_t, k_t, v_t, kk_t, a_t = inp
        sab = jnp.einsum("bhvi,bhi->bhv", S, -kk_t)
        S = (S * w_t[:, :, None, :] + sab[..., None] * (kk_t * a_t)[:, :, None, :]
             + v_t[..., None] * k_t[:, :, None, :])
        return S, jnp.einsum("bhvi,bhi->bhv", S, r_t)

    xs = tuple(jnp.moveaxis(t, 1, 0) for t in (rh, heads(decay), kh, vh, kk, ah))
    S0 = jnp.zeros((B, H, N, N), jnp.float32)
    _, o = lax.scan(step, S0, xs)
    o = jnp.moveaxis(o, 0, 1)
    mean = jnp.mean(o, axis=-1, keepdims=True)
    var = jnp.mean(jnp.square(o - mean), axis=-1, keepdims=True)
    o = ((o - mean) * lax.rsqrt(var + RWKV_GN_EPS)).reshape(B, T, D_RWKV)
    o = o * gn_w.astype(jnp.float32) + gn_b.astype(jnp.float32)
    bonus = jnp.sum(rh * kh * r_k.astype(jnp.float32), axis=-1, keepdims=True) * vh
    o = o + bonus.reshape(B, T, D_RWKV)
    return (o * g.astype(jnp.float32)).astype(r.dtype)


def peer(x, w_query, q_gain, sub_keys, u_table, v_table):
    B, T, D = x.shape
    n = B * T
    n_pad = -(-n // PEER_BLOCK) * PEER_BLOCK
    xb_all = jnp.pad(x.reshape(n, D), ((0, n_pad - n), (0, 0))).reshape(n_pad // PEER_BLOCK, PEER_BLOCK, D)

    def block(xb):
        q = rmsnorm((xb @ w_query).reshape(PEER_BLOCK, PEER_HEADS, PEER_D_QUERY), q_gain)
        s1 = jnp.einsum("nhd,hkd->nhk", q[..., :PEER_HALF], sub_keys[:, 0])
        s2 = jnp.einsum("nhd,hkd->nhk", q[..., PEER_HALF:], sub_keys[:, 1])
        v1, i1 = lax.top_k(s1, PEER_TOPK)
        v2, i2 = lax.top_k(s2, PEER_TOPK)
        cand = (v1[..., :, None] + v2[..., None, :]).reshape(PEER_BLOCK, PEER_HEADS, PEER_TOPK * PEER_TOPK)
        sc, ci = lax.top_k(cand, PEER_TOPK)
        e1 = jnp.take_along_axis(i1, ci // PEER_TOPK, axis=-1)
        e2 = jnp.take_along_axis(i2, ci % PEER_TOPK, axis=-1)
        expert = e1 * PEER_N_KEYS + e2
        gate = jax.nn.softmax(sc.astype(jnp.float32), axis=-1).astype(xb.dtype)
        u = u_table[expert]
        hid = jax.nn.gelu(jnp.einsum("nhkd,nd->nhk", u, xb), approximate=False)
        return jnp.einsum("nhk,nhkd->nd", gate * hid, v_table[expert])

    y = lax.map(block, xb_all)
    return y.reshape(n_pad, D)[:n].reshape(B, T, D)


def setup_inputs(seed: int = 0) -> dict:
    key = jax.random.key(seed)
    ks = jax.random.split(key, 32)
    f32 = jnp.float32
    L = DEPTH

    def nrm(k, shape, scale):
        return jax.random.normal(k, shape, f32) * scale

    a_init = jax.random.uniform(ks[8], (L, D_LRU), f32, 0.9, 0.999)
    return {
        "x": nrm(ks[0], (BATCH, SEQ, D_MODEL), 1.0),
        "meta_tokens": nrm(ks[1], (N_META, D_MODEL), 1.0),
        "norm1_gain": 1.0 + nrm(ks[2], (L, D_MODEL), 0.01),
        "w_in": nrm(ks[3], (L, D_MODEL, D_IN), D_MODEL ** -0.5),
        "conv_w": nrm(ks[4], (L, CONV_WIDTH, D_LRU), CONV_WIDTH ** -0.5),
        "conv_b": nrm(ks[5], (L, D_LRU), 0.01),
        "lru_gate_a_w": nrm(ks[6], (L, LRU_HEADS, LRU_HEAD_DIM, LRU_HEAD_DIM), LRU_HEAD_DIM ** -0.5),
        "lru_gate_a_b": nrm(ks[7], (L, D_LRU), 0.01),
        "lru_gate_x_w": nrm(ks[9], (L, LRU_HEADS, LRU_HEAD_DIM, LRU_HEAD_DIM), LRU_HEAD_DIM ** -0.5),
        "lru_gate_x_b": nrm(ks[10], (L, D_LRU), 0.01),
        "lru_lambda": jnp.log(a_init) - jnp.log1p(-a_init),
        "lru_out_gain": 1.0 + nrm(ks[11], (L, D_LRU), 0.01),
        "rwkv_shift_mu": jax.random.uniform(ks[12], (L, D_RWKV_IN), f32),
        "rwkv_w0": jax.random.uniform(ks[13], (L, D_RWKV), f32, -6.0, 1.0),
        "rwkv_w2": nrm(ks[14], (L, D_DECAY_LORA, D_RWKV), 0.5 * D_DECAY_LORA ** -0.5),
        "rwkv_a0": nrm(ks[15], (L, D_RWKV), 0.1),
        "rwkv_a2": nrm(ks[16], (L, D_AAA_LORA, D_RWKV), 0.5 * D_AAA_LORA ** -0.5),
        "rwkv_g2": nrm(ks[17], (L, D_GATE_LORA, D_RWKV), D_GATE_LORA ** -0.5),
        "rwkv_k_k": 0.85 + nrm(ks[18], (L, D_RWKV), 0.05),
        "rwkv_k_a": 1.0 + nrm(ks[19], (L, D_RWKV), 0.05),
        "rwkv_r_k": nrm(ks[20], (L, RWKV_HEADS, RWKV_HEAD_DIM), 0.1),
        "rwkv_gn_w": 1.0 + nrm(ks[21], (L, D_RWKV), 0.01),
        "rwkv_gn_b": nrm(ks[22], (L, D_RWKV), 0.01),
        "w_out": nrm(ks[23], (L, D_MIX, D_MODEL), D_MIX ** -0.5),
        "norm2_gain": 1.0 + nrm(ks[24], (L, D_MODEL), 0.01),
        "peer_w_query": nrm(ks[25], (L, D_MODEL, PEER_HEADS * PEER_D_QUERY), D_MODEL ** -0.5),
        "peer_q_gain": 1.0 + nrm(ks[26], (L, PEER_HEADS, PEER_D_QUERY), 0.01),
        "peer_sub_keys": nrm(ks[27], (L, PEER_HEADS, 2, PEER_N_KEYS, PEER_HALF), PEER_HALF ** -0.5),
        "peer_u": nrm(ks[28], (L, PEER_N_EXPERTS, D_MODEL), D_MODEL ** -0.5),
        "peer_v": nrm(ks[29], (L, PEER_N_EXPERTS, D_MODEL), 0.5),
        "final_norm_gain": 1.0 + nrm(ks[30], (D_MODEL,), 0.01),
    }


def reference(x, meta_tokens, norm1_gain, w_in, conv_w, conv_b, lru_gate_a_w, lru_gate_a_b,
              lru_gate_x_w, lru_gate_x_b, lru_lambda, lru_out_gain, rwkv_shift_mu, rwkv_w0,
              rwkv_w2, rwkv_a0, rwkv_a2, rwkv_g2, rwkv_k_k, rwkv_k_a, rwkv_r_k, rwkv_gn_w,
              rwkv_gn_b, w_out, norm2_gain, peer_w_query, peer_q_gain, peer_sub_keys,
              peer_u, peer_v, final_norm_gain):
    B = x.shape[0]
    meta = jnp.broadcast_to(meta_tokens[None].astype(x.dtype), (B, N_META, D_MODEL))
    h = jnp.concatenate([meta, x], axis=1)
    for l in range(DEPTH):
        xn = rmsnorm(h, norm1_gain[l])
        p = xn @ w_in[l]
        x_lru, gate_lru, p_rwkv = p[..., :D_LRU], p[..., D_LRU:2 * D_LRU], p[..., 2 * D_LRU:]
        xc = causal_depthwise_conv(x_lru, conv_w[l], conv_b[l])
        y_lru = rg_lru(xc, lru_gate_a_w[l], lru_gate_a_b[l], lru_gate_x_w[l], lru_gate_x_b[l],
                       lru_lambda[l]) * jax.nn.gelu(gate_lru)
        y_lru = rmsnorm(y_lru, lru_out_gain[l])
        p_rwkv = token_shift(p_rwkv, rwkv_shift_mu[l])
        r, k, v, w_lr, a_lr, g_lr = jnp.split(p_rwkv, RWKV_SPLITS, axis=-1)
        y_rwkv = rwkv7_time_mix(r, k, v, w_lr, a_lr, g_lr, rwkv_w0[l], rwkv_w2[l], rwkv_a0[l],
                                rwkv_a2[l], rwkv_g2[l], rwkv_k_k[l], rwkv_k_a[l], rwkv_r_k[l],
                                rwkv_gn_w[l], rwkv_gn_b[l])
        h = h + jnp.concatenate([y_lru, y_rwkv], axis=-1) @ w_out[l]
        h = h + peer(rmsnorm(h, norm2_gain[l]), peer_w_query[l], peer_q_gain[l],
                     peer_sub_keys[l], peer_u[l], peer_v[l])
    return rmsnorm(h, final_norm_gain)[:, N_META:]
```

```python
import functools

import jax
import jax.numpy as jnp
from jax import lax
from jax.experimental import pallas as pl
from jax.experimental.pallas import tpu as pltpu

F32 = jnp.float32
BF16 = jnp.bfloat16

D_MODEL = 1024
N_META = 16
NORM_EPS = 1e-6
D_LRU = 512
D_RWKV = 512
LRU_HEADS = 8
LRU_HEAD_DIM = D_LRU // LRU_HEADS
LRU_C = 8.0
RWKV_HEAD_DIM = 64
RWKV_HEADS = D_RWKV // RWKV_HEAD_DIM
D_DECAY_LORA = 64
D_AAA_LORA = 64
D_GATE_LORA = 128
RWKV_GN_EPS = 64e-5
D_RWKV_IN = 3 * D_RWKV + D_DECAY_LORA + D_AAA_LORA + D_GATE_LORA
PEER_HEADS = 8
PEER_N_KEYS = 128
PEER_D_QUERY = 256
PEER_HALF = PEER_D_QUERY // 2
PEER_TOPK = 16
N_EXPERTS = PEER_N_KEYS * PEER_N_KEYS

MIX_BLOCK = 128
FRONT_PAD = MIX_BLOCK - N_META
CHUNK = 64
INPROJ_ROWS = 256
POST_ROWS = 256
EXP_TOKENS = 512
EXP_I = 8
LANE = 128
NEG_INF = float("-inf")
VMEM_LIMIT = 52 * 1024 * 1024


def _dot(a, b):
    return jnp.dot(a.astype(BF16), b.astype(BF16), preferred_element_type=F32)


def _dot_nt(a, b):
    return lax.dot_general(a.astype(BF16), b.astype(BF16), (((1,), (1,)), ((), ())),
                           preferred_element_type=F32)


def _dot_tn(a, b):
    return lax.dot_general(a.astype(BF16), b.astype(BF16), (((0,), (0,)), ((), ())),
                           preferred_element_type=F32)


def _split2(x):
    hi = x.astype(BF16)
    lo = (x - hi.astype(F32)).astype(BF16)
    return hi, lo


def _split3(x):
    hi = x.astype(BF16)
    r1 = x - hi.astype(F32)
    mid = r1.astype(BF16)
    lo = (r1 - mid.astype(F32)).astype(BF16)
    return hi, mid, lo


def _dot3(a, b):
    ah, al = _split2(a)
    bh, bl = _split2(b)
    return (jnp.dot(ah, bh, preferred_element_type=F32)
            + jnp.dot(al, bh, preferred_element_type=F32)
            + jnp.dot(ah, bl, preferred_element_type=F32))


def _dot3_nt(a, b):
    ah, al = _split2(a)
    bh, bl = _split2(b)
    dn = (((1,), (1,)), ((), ()))
    return (lax.dot_general(ah, bh, dn, preferred_element_type=F32)
            + lax.dot_general(al, bh, dn, preferred_element_type=F32)
            + lax.dot_general(ah, bl, dn, preferred_element_type=F32))


def _group_sum(x, ones_bd):
    hi, lo = _split2(x)
    return (jnp.dot(hi, ones_bd, preferred_element_type=F32)
            + jnp.dot(lo, ones_bd, preferred_element_type=F32))


def _softplus(x):
    return jnp.maximum(x, 0.0) + jnp.log1p(jnp.exp(-jnp.abs(x)))


def _sigmoid(x):
    return 1.0 / (1.0 + jnp.exp(-x))


def _gelu_tanh(x):
    return 0.5 * x * (1.0 + jnp.tanh(0.7978845608028654 * (x + 0.044715 * (x * x * x))))


def _gelu_erf(x):
    return 0.5 * x * (1.0 + lax.erf(x * 0.7071067811865476))


def _inproj_body(h_ref, g_ref, wl_ref, wr_ref, ol_ref, or_ref):
    x = h_ref[...]
    ms = jnp.mean(x * x, axis=-1, keepdims=True)
    xn = ((x * lax.rsqrt(ms + NORM_EPS)) * g_ref[...]).astype(BF16)
    ol_ref[...] = jnp.dot(xn, wl_ref[...], preferred_element_type=F32)
    or_ref[...] = jnp.dot(xn, wr_ref[...], preferred_element_type=F32)


def _inproj(h, gain, w_lru, w_rwkv):
    n = h.shape[0]
    tm = INPROJ_ROWS
    const = lambda i: (0, 0)
    return pl.pallas_call(
        _inproj_body,
        grid=(n // tm,),
        in_specs=[
            pl.BlockSpec((tm, D_MODEL), lambda i: (i, 0)),
            pl.BlockSpec((1, D_MODEL), const),
            pl.BlockSpec((D_MODEL, 2 * D_LRU), const),
            pl.BlockSpec((D_MODEL, D_RWKV_IN), const),
        ],
        out_specs=[
            pl.BlockSpec((tm, 2 * D_LRU), lambda i: (i, 0)),
            pl.BlockSpec((tm, D_RWKV_IN), lambda i: (i, 0)),
        ],
        out_shape=[
            jax.ShapeDtypeStruct((n, 2 * D_LRU), F32),
            jax.ShapeDtypeStruct((n, D_RWKV_IN), F32),
        ],
        compiler_params=pltpu.CompilerParams(
            dimension_semantics=("arbitrary",), vmem_limit_bytes=VMEM_LIMIT),
        name="inproj",
    )(h, gain, w_lru, w_rwkv)


def _lru_body(p_ref, cw_ref, cb_ref, wa_ref, ba_ref, wx_ref, bx_ref, lam_ref, gain_ref,
              o_ref, xext_ref, h_ref):
    t = pl.program_id(1)
    tb = MIX_BLOCK

    @pl.when(t == 0)
    def _():
        xext_ref[0:8, :] = jnp.zeros((8, D_LRU), F32)
        h_ref[...] = jnp.zeros_like(h_ref)

    x = p_ref[0, :, 0:D_LRU]
    gate = p_ref[0, :, D_LRU:2 * D_LRU]
    xext_ref[8:8 + tb, :] = x
    cw = cw_ref[...]
    xc = (cb_ref[...] + cw[3:4, :] * x
          + cw[2:3, :] * xext_ref[7:7 + tb, :]
          + cw[1:2, :] * xext_ref[6:6 + tb, :]
          + cw[0:1, :] * xext_ref[5:5 + tb, :])
    xext_ref[0:8, :] = x[tb - 8:tb, :]

    xcb = xc.astype(BF16)
    r = _sigmoid(jnp.dot(xcb, wa_ref[...], preferred_element_type=F32) + ba_ref[...])
    i = _sigmoid(jnp.dot(xcb, wx_ref[...], preferred_element_type=F32) + bx_ref[...])
    log_a = (-LRU_C) * r * _softplus(-lam_ref[...])
    a = jnp.exp(log_a)
    th = jnp.tanh(log_a)
    one_minus_a2 = (-2.0 * th) / (1.0 - th)
    b = jnp.sqrt(one_minus_a2) * (i * xc)
    rows = lax.broadcasted_iota(jnp.int32, (tb, D_LRU), 0)
    b = jnp.where(rows + t * tb >= FRONT_PAD, b, 0.0)

    d = 1
    while d < tb:
        keep = rows >= d
        a_sh = jnp.where(keep, pltpu.roll(a, d, axis=0), 1.0)
        b_sh = jnp.where(keep, pltpu.roll(b, d, axis=0), 0.0)
        b = a * b_sh + b
        a = a * a_sh
        d *= 2
    h = a * h_ref[...] + b
    h_ref[...] = h[tb - 1:tb, :]

    y = h * _gelu_tanh(gate)
    ms = jnp.mean(y * y, axis=-1, keepdims=True)
    o_ref[0] = ((y * lax.rsqrt(ms + NORM_EPS)) * gain_ref[...]).astype(BF16)


def _lru(p_lru, conv_w, conv_b, wa_bd, ba, wx_bd, bx, lam, gain, seq):
    bsz, t_pad, _ = p_lru.shape
    nt = t_pad // MIX_BLOCK
    c2 = lambda b, t: (0, 0)
    vec = pl.BlockSpec((1, D_LRU), c2)
    mat = pl.BlockSpec((D_LRU, D_LRU), c2)
    return pl.pallas_call(
        _lru_body,
        grid=(bsz, nt),
        in_specs=[
            pl.BlockSpec((1, MIX_BLOCK, 2 * D_LRU), lambda b, t: (b, t, 0)),
            pl.BlockSpec((4, D_LRU), c2), vec, mat, vec, mat, vec, vec, vec,
        ],
        out_specs=pl.BlockSpec((1, MIX_BLOCK, D_LRU), lambda b, t: (b, jnp.maximum(t - 1, 0), 0)),
        out_shape=jax.ShapeDtypeStruct((bsz, seq, D_LRU), BF16),
        scratch_shapes=[pltpu.VMEM((MIX_BLOCK + 8, D_LRU), F32), pltpu.VMEM((1, D_LRU), F32)],
        compiler_params=pltpu.CompilerParams(
            dimension_semantics=("arbitrary", "arbitrary"), vmem_limit_bytes=VMEM_LIMIT),
        name="lru",
    )(p_lru, conv_w, conv_b, wa_bd, ba, wx_bd, bx, lam, gain)


def _rwkv_body(p_ref, mu_ref, w0_ref, lora_ref, a0_ref, g2_ref, kk_ref, ka_ref, rk_ref,
               gnw_ref, gnb_ref, ones_ref, tri_ref, o_ref, carry_ref, s_ref):
    t = pl.program_id(1)
    tb = MIX_BLOCK
    hd = RWKV_HEAD_DIM

    @pl.when(t == 0)
    def _():
        carry_ref[...] = jnp.zeros_like(carry_ref)
        s_ref[...] = jnp.zeros_like(s_ref)

    p = p_ref[0]
    rows = lax.broadcasted_iota(jnp.int32, (tb, D_RWKV_IN), 0)
    prev = jnp.where(rows == 0, carry_ref[...], pltpu.roll(p, 1, axis=0))
    carry_ref[...] = p[tb - 1:tb, :]
    ps = p + (prev - p) * mu_ref[...]

    r = ps[:, 0:D_RWKV]
    k = ps[:, D_RWKV:2 * D_RWKV]
    v = ps[:, 2 * D_RWKV:3 * D_RWKV]
    lo = ps[:, 3 * D_RWKV:3 * D_RWKV + 128]
    gl = ps[:, 3 * D_RWKV + 128:D_RWKV_IN]
    lane = lax.broadcasted_iota(jnp.int32, (tb, 128), 1)
    lo_act = jnp.where(lane < D_DECAY_LORA, jnp.tanh(lo), lo)
    la = _dot(lo_act, lora_ref[...])
    log_w = -_softplus(-(w0_ref[...] + la[:, 0:D_RWKV])) - 0.5
    ld = -jnp.exp(log_w)
    a = _sigmoid(a0_ref[...] + la[:, D_RWKV:2 * D_RWKV])
    g = _dot(_sigmoid(gl), g2_ref[...])

    ones_bd = ones_ref[...]
    kkr = k * kk_ref[...]
    kk = kkr / jnp.maximum(jnp.sqrt(_group_sum(kkr * kkr, ones_bd)), 1e-12)
    k2 = k * (1.0 + (a - 1.0) * ka_ref[...])
    bonus = _group_sum(r * k2 * rk_ref[...], ones_bd) * v
    kka = kk * a

    tri = tri_ref[...]
    ri = lax.broadcasted_iota(jnp.int32, (CHUNK, CHUNK), 0)
    ci = lax.broadcasted_iota(jnp.int32, (CHUNK, CHUNK), 1)
    strict = ri > ci
    incl = ri >= ci
    eye = ri == ci

    for c in range(tb // CHUNK):
        sl = slice(c * CHUNK, (c + 1) * CHUNK)
        ld_c = ld[sl]
        hi, mid, lw = _split3(ld_c)
        cum = (jnp.dot(tri, hi, preferred_element_type=F32)
               + jnp.dot(tri, mid, preferred_element_type=F32)
               + jnp.dot(tri, lw, preferred_element_type=F32))
        ltot = cum[CHUNK - 1:CHUNK, :]
        e_inv = jnp.exp(-cum)
        e_end = jnp.exp(ltot - cum)
        at = -kk[sl] * jnp.exp(cum - ld_c)
        rt = r[sl] * jnp.exp(cum)
        bt = kka[sl] * e_inv
        kt = k2[sl] * e_inv
        bhat = kka[sl] * e_end
        khat = k2[sl] * e_end
        wtot = jnp.exp(ltot)
        vc = v[sl]

        outs = []
        for h in range(RWKV_HEADS):
            hs = slice(h * hd, (h + 1) * hd)
            at_h, rt_h, v_h = at[:, hs], rt[:, hs], vc[:, hs]
            gram = _dot_nt(jnp.concatenate([at_h, rt_h], axis=0),
                           jnp.concatenate([bt[:, hs], kt[:, hs]], axis=0))
            n_ab = jnp.where(strict, gram[0:CHUNK, 0:CHUNK], 0.0)
            a_ak = jnp.where(strict, gram[0:CHUNK, CHUNK:2 * CHUNK], 0.0)
            a_rb = jnp.where(incl, gram[CHUNK:2 * CHUNK, 0:CHUNK], 0.0)
            a_rk = jnp.where(incl, gram[CHUNK:2 * CHUNK, CHUNK:2 * CHUNK], 0.0)
            tinv = jnp.where(eye, 1.0, n_ab)
            pw = n_ab
            for _ in range(5):
                pw = _dot(pw, pw)
                tinv = tinv + _dot(tinv, pw)
            uu = _dot(tinv, jnp.concatenate([at_h, _dot(a_ak, v_h)], axis=1))
            ro = _dot(a_rb, uu)
            ra = rt_h + ro[:, 0:hd]
            ov = ro[:, hd:2 * hd] + _dot(a_rk, v_h)
            ms = _dot_tn(bhat[:, hs], uu)
            m_h = ms[:, 0:hd] + jnp.where(eye, wtot[:, hs], 0.0)
            sv = ms[:, hd:2 * hd] + _dot_tn(khat[:, hs], v_h)
            s0 = s_ref[:, hs]
            outs.append(_dot3(ra, s0) + ov)
            s_ref[:, hs] = _dot3(m_h, s0) + sv

        o = jnp.concatenate(outs, axis=1)
        mean = _group_sum(o, ones_bd) * (1.0 / hd)
        dev = o - mean
        var = _group_sum(dev * dev, ones_bd) * (1.0 / hd)
        on = dev * lax.rsqrt(var + RWKV_GN_EPS) * gnw_ref[...] + gnb_ref[...] + bonus[sl]
        o_ref[0, sl, :] = (on * g[sl]).astype(BF16)


def _rwkv(p_rwkv, mu, w0, lora_bd, a0, g2, k_k, k_a, r_k, gn_w, gn_b, ones_bd, tri, seq):
    bsz, t_pad, _ = p_rwkv.shape
    nt = t_pad // MIX_BLOCK
    c2 = lambda b, t: (0, 0)
    vec = pl.BlockSpec((1, D_RWKV), c2)
    return pl.pallas_call(
        _rwkv_body,
        grid=(bsz, nt),
        in_specs=[
            pl.BlockSpec((1, MIX_BLOCK, D_RWKV_IN), lambda b, t: (b, t, 0)),
            pl.BlockSpec((1, D_RWKV_IN), c2), vec,
            pl.BlockSpec((128, 2 * D_RWKV), c2), vec,
            pl.BlockSpec((D_GATE_LORA, D_RWKV), c2), vec, vec, vec, vec, vec,
            pl.BlockSpec((D_RWKV, D_RWKV), c2),
            pl.BlockSpec((CHUNK, CHUNK), c2),
        ],
        out_specs=pl.BlockSpec((1, MIX_BLOCK, D_RWKV), lambda b, t: (b, jnp.maximum(t - 1, 0), 0)),
        out_shape=jax.ShapeDtypeStruct((bsz, seq, D_RWKV), BF16),
        scratch_shapes=[pltpu.VMEM((1, D_RWKV_IN), F32), pltpu.VMEM((RWKV_HEAD_DIM, D_RWKV), F32)],
        compiler_params=pltpu.CompilerParams(
            dimension_semantics=("arbitrary", "arbitrary"), vmem_limit_bytes=VMEM_LIMIT),
        name="rwkv",
    )(p_rwkv, mu, w0, lora_bd, a0, g2, k_k, k_a, r_k, gn_w, gn_b, ones_bd, tri)


_CAND_COUNTS = tuple(PEER_TOPK // (a + 1) for a in range(PEER_TOPK))
_CAND_ROWS = -(-sum(_CAND_COUNTS) // 8) * 8


def _post_body(yl_ref, yr_ref, x_ref, woa_ref, wob_ref, g2_ref, wq_ref, qg_ref, k1_ref, k2_ref,
               h1_ref, xt_ref, s1_ref, s2_ref, st_ref, q_s, v1_s, v2_s, cand_s):
    tm = POST_ROWS
    h1 = (x_ref[...] + jnp.dot(yl_ref[...], woa_ref[...], preferred_element_type=F32)
          + jnp.dot(yr_ref[...], wob_ref[...], preferred_element_type=F32))
    h1_ref[...] = h1
    ms = jnp.mean(h1 * h1, axis=-1, keepdims=True)
    xn = (h1 * lax.rsqrt(ms + NORM_EPS)) * g2_ref[...]
    xt_ref[...] = xn.T.astype(BF16)
    q_s[...] = jnp.dot(xn.astype(BF16), wq_ref[...], preferred_element_type=F32)
    cand_s[...] = jnp.full(cand_s.shape, NEG_INF, F32)

    def head(h, carry):
        off = pl.multiple_of(h * PEER_D_QUERY, PEER_D_QUERY)
        qh = q_s[:, pl.ds(off, PEER_D_QUERY)]
        msq = jnp.mean(qh * qh, axis=-1, keepdims=True)
        qn = (qh * lax.rsqrt(msq + NORM_EPS)) * qg_ref[:, pl.ds(off, PEER_D_QUERY)]
        for half, (k_ref, s_ref, v_s) in enumerate(((k1_ref, s1_ref, v1_s), (k2_ref, s2_ref, v2_s))):
            s = _dot3_nt(k_ref[h], qn[:, half * PEER_HALF:(half + 1) * PEER_HALF])
            s_ref[h] = s
            for a in range(PEER_TOPK):
                m = jnp.max(s, axis=0, keepdims=True)
                v_s[a:a + 1, :] = m
                s = jnp.where(s == m, NEG_INF, s)
        row = 0
        for a, nb in enumerate(_CAND_COUNTS):
            cand_s[row:row + nb, :] = v1_s[a:a + 1, :] + v2_s[0:nb, :]
            row += nb
        cand = cand_s[...]
        m0 = jnp.max(cand, axis=0, keepdims=True)
        m = m0
        z = jnp.zeros((1, tm), F32)
        for it in range(PEER_TOPK):
            if it:
                m = jnp.max(cand, axis=0, keepdims=True)
            z = z + jnp.exp(m - m0)
            cand = jnp.where(cand == m, NEG_INF, cand)
        st_ref[h] = jnp.concatenate(
            [m, v1_s[0:1, :], v2_s[0:1, :], 1.0 / z, jnp.zeros((4, tm), F32)], axis=0)
        return carry

    lax.fori_loop(0, PEER_HEADS, head, 0)


def _post(y_lru, y_rwkv, x, wo_a, wo_b, g2, wq, qg, k1, k2):
    n = x.shape[0]
    tm = POST_ROWS
    c2 = lambda i: (0, 0)
    c3 = lambda i: (0, 0, 0)
    hk = (PEER_HEADS, PEER_N_KEYS)
    return pl.pallas_call(
        _post_body,
        grid=(n // tm,),
        in_specs=[
            pl.BlockSpec((tm, D_LRU), lambda i: (i, 0)),
            pl.BlockSpec((tm, D_RWKV), lambda i: (i, 0)),
            pl.BlockSpec((tm, D_MODEL), lambda i: (i, 0)),
            pl.BlockSpec((D_LRU, D_MODEL), c2),
            pl.BlockSpec((D_RWKV, D_MODEL), c2),
            pl.BlockSpec((1, D_MODEL), c2),
            pl.BlockSpec((D_MODEL, PEER_HEADS * PEER_D_QUERY), c2),
            pl.BlockSpec((1, PEER_HEADS * PEER_D_QUERY), c2),
            pl.BlockSpec(hk + (PEER_HALF,), c3),
            pl.BlockSpec(hk + (PEER_HALF,), c3),
        ],
        out_specs=[
            pl.BlockSpec((tm, D_MODEL), lambda i: (i, 0)),
            pl.BlockSpec((D_MODEL, tm), lambda i: (0, i)),
            pl.BlockSpec(hk + (tm,), lambda i: (0, 0, i)),
            pl.BlockSpec(hk + (tm,), lambda i: (0, 0, i)),
            pl.BlockSpec((PEER_HEADS, 8, tm), lambda i: (0, 0, i)),
        ],
        out_shape=[
            jax.ShapeDtypeStruct((n, D_MODEL), F32),
            jax.ShapeDtypeStruct((D_MODEL, n), BF16),
            jax.ShapeDtypeStruct(hk + (n,), F32),
            jax.ShapeDtypeStruct(hk + (n,), F32),
            jax.ShapeDtypeStruct((PEER_HEADS, 8, n), F32),
        ],
        scratch_shapes=[
            pltpu.VMEM((tm, PEER_HEADS * PEER_D_QUERY), F32),
            pltpu.VMEM((PEER_TOPK, tm), F32),
            pltpu.VMEM((PEER_TOPK, tm), F32),
            pltpu.VMEM((_CAND_ROWS, tm), F32),
        ],
        compiler_params=pltpu.CompilerParams(
            dimension_semantics=("arbitrary",), vmem_limit_bytes=VMEM_LIMIT),
        name="post",
    )(y_lru, y_rwkv, x, wo_a, wo_b, g2, wq, qg, k1, k2)


def _experts_body(xt_ref, s1_ref, s2_ref, st_ref, u_ref, vt_ref, h1_ref, fg_ref, o_ref,
                  acc_ref, e1_ref, e2_ref, ht_ref, pt_ref):
    e = pl.program_id(1)
    tn = EXP_TOKENS
    nlb = tn // LANE

    @pl.when(e == 0)
    def _():
        acc_ref[...] = jnp.zeros_like(acc_ref)
        for h in range(PEER_HEADS):
            e1_ref[h] = jnp.exp(s1_ref[h] - st_ref[h, 1:2, :]) * st_ref[h, 3:4, :]
            e2_ref[h] = jnp.exp(s2_ref[h] - st_ref[h, 2:3, :])

    ht_ref[...] = jnp.dot(u_ref[...], xt_ref[...], preferred_element_type=F32)

    i0 = pl.multiple_of(e * EXP_I, EXP_I)

    def piece(lb, carry):
        lanes = pl.ds(pl.multiple_of(lb * LANE, LANE), LANE)
        for ii in range(EXP_I):
            rows = slice(ii * PEER_N_KEYS, (ii + 1) * PEER_N_KEYS)
            gate = jnp.zeros((PEER_N_KEYS, LANE), F32)
            for h in range(PEER_HEADS):
                s1_row = s1_ref[h, pl.ds(i0, EXP_I), lanes][ii:ii + 1, :]
                e1_row = e1_ref[h, pl.ds(i0, EXP_I), lanes][ii:ii + 1, :]
                c = s2_ref[h, :, lanes] + s1_row
                val = e2_ref[h, :, lanes] * e1_row
                gate = gate + jnp.where(c >= st_ref[h, 0:1, lanes], val, 0.0)
            pt_ref[rows, lanes] = (gate * _gelu_erf(ht_ref[rows, lanes])).astype(BF16)
        return carry

    lax.fori_loop(0, nlb, piece, 0)
    acc_ref[...] += jnp.dot(vt_ref[...], pt_ref[...], preferred_element_type=F32)

    @pl.when(e == pl.num_programs(1) - 1)
    def _():
        h2 = h1_ref[...] + acc_ref[...].T
        ms = jnp.mean(h2 * h2, axis=-1, keepdims=True)
        o_ref[...] = (h2 * lax.rsqrt(ms + NORM_EPS)) * fg_ref[...]


def _experts(xt, s1, s2, st, u_bf, vt_bf, h1, fgain):
    n = h1.shape[0]
    tn = EXP_TOKENS
    te = EXP_I * PEER_N_KEYS
    hk = (PEER_HEADS, PEER_N_KEYS)
    return pl.pallas_call(
        _experts_body,
        grid=(n // tn, N_EXPERTS // te),
        in_specs=[
            pl.BlockSpec((D_MODEL, tn), lambda i, e: (0, i)),
            pl.BlockSpec(hk + (tn,), lambda i, e: (0, 0, i)),
            pl.BlockSpec(hk + (tn,), lambda i, e: (0, 0, i)),
            pl.BlockSpec((PEER_HEADS, 8, tn), lambda i, e: (0, 0, i)),
            pl.BlockSpec((te, D_MODEL), lambda i, e: (e, 0)),
            pl.BlockSpec((D_MODEL, te), lambda i, e: (0, e)),
            pl.BlockSpec((tn, D_MODEL), lambda i, e: (i, 0)),
            pl.BlockSpec((1, D_MODEL), lambda i, e: (0, 0)),
        ],
        out_specs=pl.BlockSpec((tn, D_MODEL), lambda i, e: (i, 0)),
        out_shape=jax.ShapeDtypeStruct((n, D_MODEL), F32),
        scratch_shapes=[
            pltpu.VMEM((D_MODEL, tn), F32),
            pltpu.VMEM(hk + (tn,), F32),
            pltpu.VMEM(hk + (tn,), F32),
            pltpu.VMEM((te, tn), F32),
            pltpu.VMEM((te, tn), BF16),
        ],
        compiler_params=pltpu.CompilerParams(
            dimension_semantics=("arbitrary", "arbitrary"), vmem_limit_bytes=VMEM_LIMIT),
        name="experts",
    )(xt, s1, s2, st, u_bf, vt_bf, h1, fgain)


def _block_diag(w):
    nh, d, _ = w.shape
    eye = jnp.eye(nh, dtype=w.dtype)
    return (eye[:, None, :, None] * w[:, :, None, :]).reshape(nh * d, nh * d)


def kernel(x, meta_tokens, norm1_gain, w_in, conv_w, conv_b, lru_gate_a_w, lru_gate_a_b, lru_gate_x_w, lru_gate_x_b, lru_lambda, lru_out_gain, rwkv_shift_mu, rwkv_w0, rwkv_w2, rwkv_a0, rwkv_a2, rwkv_g2, rwkv_k_k, rwkv_k_a, rwkv_r_k, rwkv_gn_w, rwkv_gn_b, w_out, norm2_gain, peer_w_query, peer_q_gain, peer_sub_keys, peer_u, peer_v, final_norm_gain):
    bsz, seq, _ = x.shape
    row = lambda v: v.reshape(1, -1).astype(F32)

    w_in_bf = w_in[0].astype(BF16)
    w_lru, w_rwkv = w_in_bf[:, :2 * D_LRU], w_in_bf[:, 2 * D_LRU:]
    wa_bd = _block_diag(lru_gate_a_w[0]).astype(BF16)
    wx_bd = _block_diag(lru_gate_x_w[0]).astype(BF16)
    zeros_l = jnp.zeros((D_DECAY_LORA, D_RWKV), F32)
    lora_bd = jnp.concatenate(
        [jnp.concatenate([rwkv_w2[0], zeros_l], axis=1),
         jnp.concatenate([zeros_l, rwkv_a2[0]], axis=1)], axis=0).astype(BF16)
    ones_bd = _block_diag(jnp.ones((RWKV_HEADS, RWKV_HEAD_DIM, RWKV_HEAD_DIM), F32)).astype(BF16)
    tri = jnp.tril(jnp.ones((CHUNK, CHUNK), F32)).astype(BF16)
    wo_bf = w_out[0].astype(BF16)
    wq_bf = peer_w_query[0].astype(BF16)
    u_bf = peer_u[0].astype(BF16)
    vt_bf = peer_v[0].T.astype(BF16)

    head = jnp.concatenate(
        [jnp.zeros((FRONT_PAD, D_MODEL), x.dtype), meta_tokens.astype(x.dtype)], axis=0)
    h0 = jnp.concatenate([jnp.broadcast_to(head[None], (bsz, MIX_BLOCK, D_MODEL)), x], axis=1)
    t_pad = seq + MIX_BLOCK
    p_lru, p_rwkv = _inproj(h0.reshape(bsz * t_pad, D_MODEL), row(norm1_gain[0]), w_lru, w_rwkv)
    y_lru = _lru(p_lru.reshape(bsz, t_pad, 2 * D_LRU), conv_w[0], row(conv_b[0]), wa_bd,
                 row(lru_gate_a_b[0]), wx_bd, row(lru_gate_x_b[0]), row(lru_lambda[0]),
                 row(lru_out_gain[0]), seq)
    y_rwkv = _rwkv(p_rwkv.reshape(bsz, t_pad, D_RWKV_IN), row(rwkv_shift_mu[0]), row(rwkv_w0[0]),
                   lora_bd, row(rwkv_a0[0]), rwkv_g2[0].astype(BF16), row(rwkv_k_k[0]),
                   row(rwkv_k_a[0]), row(rwkv_r_k[0]), row(rwkv_gn_w[0]), row(rwkv_gn_b[0]),
                   ones_bd, tri, seq)

    n = bsz * seq
    h1, xt, s1, s2, st = _post(
        y_lru.reshape(n, D_LRU), y_rwkv.reshape(n, D_RWKV), x.reshape(n, D_MODEL),
        wo_bf[:D_LRU], wo_bf[D_LRU:], row(norm2_gain[0]), wq_bf, row(peer_q_gain[0]),
        peer_sub_keys[0, :, 0], peer_sub_keys[0, :, 1])
    out = _experts(xt, s1, s2, st, u_bf, vt_bf, h1, row(final_norm_gain))
    return out.reshape(bsz, seq, D_MODEL)
```

```python
import functools

import jax
import jax.numpy as jnp
from jax import lax
from jax.experimental import pallas as pl
from jax.experimental.pallas import tpu as pltpu

F32 = jnp.float32
BF16 = jnp.bfloat16
U32 = jnp.uint32

D_MODEL = 1024
N_META = 16
NORM_EPS = 1e-6
D_LRU = 512
D_RWKV = 512
LRU_HEADS = 8
LRU_HEAD_DIM = D_LRU // LRU_HEADS
LRU_C = 8.0
RWKV_HEAD_DIM = 64
RWKV_HEADS = D_RWKV // RWKV_HEAD_DIM
D_DECAY_LORA = 64
D_AAA_LORA = 64
D_GATE_LORA = 128
RWKV_GN_EPS = 64e-5
D_RWKV_IN = 3 * D_RWKV + D_DECAY_LORA + D_AAA_LORA + D_GATE_LORA
PEER_HEADS = 8
PEER_N_KEYS = 128
PEER_D_QUERY = 256
PEER_HALF = PEER_D_QUERY // 2
PEER_TOPK = 16
N_EXPERTS = PEER_N_KEYS * PEER_N_KEYS

MIX_BLOCK = 128
FRONT_PAD = MIX_BLOCK - N_META
CHUNK = 64
INPROJ_ROWS = 256
POST_ROWS = 256
EXP_TOKENS = 512
EXP_I = 8
LANE = 128
NEG_INF = float("-inf")
VMEM_LIMIT = 52 * 1024 * 1024


def _dot(a, b):
    return jnp.dot(a.astype(BF16), b.astype(BF16), preferred_element_type=F32)


def _dot_nt(a, b):
    return lax.dot_general(a.astype(BF16), b.astype(BF16), (((1,), (1,)), ((), ())),
                           preferred_element_type=F32)


def _dot_tn(a, b):
    return lax.dot_general(a.astype(BF16), b.astype(BF16), (((0,), (0,)), ((), ())),
                           preferred_element_type=F32)


def _split2(x):
    hi = x.astype(BF16)
    lo = (x - hi.astype(F32)).astype(BF16)
    return hi, lo


def _split3(x):
    hi = x.astype(BF16)
    r1 = x - hi.astype(F32)
    mid = r1.astype(BF16)
    lo = (r1 - mid.astype(F32)).astype(BF16)
    return hi, mid, lo


def _dot3(a, b):
    ah, al = _split2(a)
    bh, bl = _split2(b)
    return (jnp.dot(ah, bh, preferred_element_type=F32)
            + jnp.dot(al, bh, preferred_element_type=F32)
            + jnp.dot(ah, bl, preferred_element_type=F32))


def _dot3_nt(a, b):
    ah, al = _split2(a)
    bh, bl = _split2(b)
    dn = (((1,), (1,)), ((), ()))
    return (lax.dot_general(ah, bh, dn, preferred_element_type=F32)
            + lax.dot_general(al, bh, dn, preferred_element_type=F32)
            + lax.dot_general(ah, bl, dn, preferred_element_type=F32))


def _group_sum(x, ones_bd):
    hi, lo = _split2(x)
    return (jnp.dot(hi, ones_bd, preferred_element_type=F32)
            + jnp.dot(lo, ones_bd, preferred_element_type=F32))


def _softplus(x):
    return jnp.maximum(x, 0.0) + jnp.log1p(jnp.exp(-jnp.abs(x)))


def _sigmoid(x):
    return 1.0 / (1.0 + jnp.exp(-x))


def _gelu_tanh(x):
    return 0.5 * x * (1.0 + jnp.tanh(0.7978845608028654 * (x + 0.044715 * (x * x * x))))


def _gelu_erf(x):
    return 0.5 * x * (1.0 + lax.erf(x * 0.7071067811865476))


def _inproj_body(h_ref, g_ref, wl_ref, wr_ref, ol_ref, or_ref):
    x = h_ref[...]
    ms = jnp.mean(x * x, axis=-1, keepdims=True)
    xn = ((x * lax.rsqrt(ms + NORM_EPS)) * g_ref[...]).astype(BF16)
    ol_ref[...] = jnp.dot(xn, wl_ref[...], preferred_element_type=F32)
    or_ref[...] = jnp.dot(xn, wr_ref[...], preferred_element_type=F32)


def _inproj(h, gain, w_lru, w_rwkv):
    n = h.shape[0]
    tm = INPROJ_ROWS
    const = lambda i: (0, 0)
    return pl.pallas_call(
        _inproj_body,
        grid=(n // tm,),
        in_specs=[
            pl.BlockSpec((tm, D_MODEL), lambda i: (i, 0)),
            pl.BlockSpec((1, D_MODEL), const),
            pl.BlockSpec((D_MODEL, 2 * D_LRU), const),
            pl.BlockSpec((D_MODEL, D_RWKV_IN), const),
        ],
        out_specs=[
            pl.BlockSpec((tm, 2 * D_LRU), lambda i: (i, 0)),
            pl.BlockSpec((tm, D_RWKV_IN), lambda i: (i, 0)),
        ],
        out_shape=[
            jax.ShapeDtypeStruct((n, 2 * D_LRU), F32),
            jax.ShapeDtypeStruct((n, D_RWKV_IN), F32),
        ],
        compiler_params=pltpu.CompilerParams(
            dimension_semantics=("arbitrary",), vmem_limit_bytes=VMEM_LIMIT),
        name="inproj",
    )(h, gain, w_lru, w_rwkv)


def _lru_body(p_ref, cw_ref, cb_ref, wa_ref, ba_ref, wx_ref, bx_ref, lam_ref, gain_ref,
              o_ref, xext_ref, h_ref):
    t = pl.program_id(1)
    tb = MIX_BLOCK

    @pl.when(t == 0)
    def _():
        xext_ref[0:8, :] = jnp.zeros((8, D_LRU), F32)
        h_ref[...] = jnp.zeros_like(h_ref)

    x = p_ref[0, :, 0:D_LRU]
    gate = p_ref[0, :, D_LRU:2 * D_LRU]
    xext_ref[8:8 + tb, :] = x
    cw = cw_ref[...]
    xc = (cb_ref[...] + cw[3:4, :] * x
          + cw[2:3, :] * xext_ref[7:7 + tb, :]
          + cw[1:2, :] * xext_ref[6:6 + tb, :]
          + cw[0:1, :] * xext_ref[5:5 + tb, :])
    xext_ref[0:8, :] = x[tb - 8:tb, :]

    xcb = xc.astype(BF16)
    r = _sigmoid(jnp.dot(xcb, wa_ref[...], preferred_element_type=F32) + ba_ref[...])
    i = _sigmoid(jnp.dot(xcb, wx_ref[...], preferred_element_type=F32) + bx_ref[...])
    log_a = (-LRU_C) * r * _softplus(-lam_ref[...])
    a = jnp.exp(log_a)
    th = jnp.tanh(log_a)
    one_minus_a2 = (-2.0 * th) / (1.0 - th)
    b = jnp.sqrt(one_minus_a2) * (i * xc)
    rows = lax.broadcasted_iota(jnp.int32, (tb, D_LRU), 0)
    b = jnp.where(rows + t * tb >= FRONT_PAD, b, 0.0)

    d = 1
    while d < tb:
        keep = rows >= d
        a_sh = jnp.where(keep, pltpu.roll(a, d, axis=0), 1.0)
        b_sh = jnp.where(keep, pltpu.roll(b, d, axis=0), 0.0)
        b = a * b_sh + b
        a = a * a_sh
        d *= 2
    h = a * h_ref[...] + b
    h_ref[...] = h[tb - 1:tb, :]

    y = h * _gelu_tanh(gate)
    ms = jnp.mean(y * y, axis=-1, keepdims=True)
    o_ref[0] = ((y * lax.rsqrt(ms + NORM_EPS)) * gain_ref[...]).astype(BF16)


def _lru(p_lru, conv_w, conv_b, wa_bd, ba, wx_bd, bx, lam, gain, seq):
    bsz, t_pad, _ = p_lru.shape
    nt = t_pad // MIX_BLOCK
    c2 = lambda b, t: (0, 0)
    vec = pl.BlockSpec((1, D_LRU), c2)
    mat = pl.BlockSpec((D_LRU, D_LRU), c2)
    return pl.pallas_call(
        _lru_body,
        grid=(bsz, nt),
        in_specs=[
            pl.BlockSpec((1, MIX_BLOCK, 2 * D_LRU), lambda b, t: (b, t, 0)),
            pl.BlockSpec((4, D_LRU), c2), vec, mat, vec, mat, vec, vec, vec,
        ],
        out_specs=pl.BlockSpec((1, MIX_BLOCK, D_LRU), lambda b, t: (b, jnp.maximum(t - 1, 0), 0)),
        out_shape=jax.ShapeDtypeStruct((bsz, seq, D_LRU), BF16),
        scratch_shapes=[pltpu.VMEM((MIX_BLOCK + 8, D_LRU), F32), pltpu.VMEM((1, D_LRU), F32)],
        compiler_params=pltpu.CompilerParams(
            dimension_semantics=("arbitrary", "arbitrary"), vmem_limit_bytes=VMEM_LIMIT),
        name="lru",
    )(p_lru, conv_w, conv_b, wa_bd, ba, wx_bd, bx, lam, gain)


def _rwkv_body(p_ref, mu_ref, w0_ref, lora_ref, a0_ref, g2_ref, kk_ref, ka_ref, rk_ref,
               gnw_ref, gnb_ref, ones_ref, tri_ref, o_ref, carry_ref, s_ref):
    t = pl.program_id(1)
    tb = MIX_BLOCK
    hd = RWKV_HEAD_DIM

    @pl.when(t == 0)
    def _():
        carry_ref[...] = jnp.zeros_like(carry_ref)
        s_ref[...] = jnp.zeros_like(s_ref)

    p = p_ref[0]
    rows = lax.broadcasted_iota(jnp.int32, (tb, D_RWKV_IN), 0)
    prev = jnp.where(rows == 0, carry_ref[...], pltpu.roll(p, 1, axis=0))
    carry_ref[...] = p[tb - 1:tb, :]
    ps = p + (prev - p) * mu_ref[...]

    r = ps[:, 0:D_RWKV]
    k = ps[:, D_RWKV:2 * D_RWKV]
    v = ps[:, 2 * D_RWKV:3 * D_RWKV]
    lo = ps[:, 3 * D_RWKV:3 * D_RWKV + 128]
    gl = ps[:, 3 * D_RWKV + 128:D_RWKV_IN]
    lane = lax.broadcasted_iota(jnp.int32, (tb, 128), 1)
    lo_act = jnp.where(lane < D_DECAY_LORA, jnp.tanh(lo), lo)
    la = _dot(lo_act, lora_ref[...])
    log_w = -_softplus(-(w0_ref[...] + la[:, 0:D_RWKV])) - 0.5
    ld = -jnp.exp(log_w)
    a = _sigmoid(a0_ref[...] + la[:, D_RWKV:2 * D_RWKV])
    g = _dot(_sigmoid(gl), g2_ref[...])

    ones_bd = ones_ref[...]
    kkr = k * kk_ref[...]
    kk = kkr / jnp.maximum(jnp.sqrt(_group_sum(kkr * kkr, ones_bd)), 1e-12)
    k2 = k * (1.0 + (a - 1.0) * ka_ref[...])
    bonus = _group_sum(r * k2 * rk_ref[...], ones_bd) * v
    kka = kk * a

    tri = tri_ref[...]
    ri = lax.broadcasted_iota(jnp.int32, (CHUNK, CHUNK), 0)
    ci = lax.broadcasted_iota(jnp.int32, (CHUNK, CHUNK), 1)
    strict = ri > ci
    incl = ri >= ci
    eye = ri == ci

    nchunk = tb // CHUNK
    at_i, rt_i, v_i, gram, bhat_t, khat_t, wtot_i = [], [], [], [], [], [], []
    for c in range(nchunk):
        sl = slice(c * CHUNK, (c + 1) * CHUNK)
        ld_c = ld[sl]
        hi, mid, lw = _split3(ld_c)
        cum = (jnp.dot(tri, hi, preferred_element_type=F32)
               + jnp.dot(tri, mid, preferred_element_type=F32)
               + jnp.dot(tri, lw, preferred_element_type=F32))
        ltot = cum[CHUNK - 1:CHUNK, :]
        e_inv = jnp.exp(-cum)
        e_end = jnp.exp(ltot - cum)
        at = -kk[sl] * jnp.exp(cum - ld_c)
        rt = r[sl] * jnp.exp(cum)
        bt = kka[sl] * e_inv
        kt = k2[sl] * e_inv
        bhat = kka[sl] * e_end
        khat = k2[sl] * e_end
        wtot = jnp.exp(ltot)
        for h in range(RWKV_HEADS):
            hs = slice(h * hd, (h + 1) * hd)
            at_i.append(at[:, hs])
            rt_i.append(rt[:, hs])
            v_i.append(v[sl, hs])
            wtot_i.append(wtot[:, hs])
            gram.append(_dot_nt(jnp.concatenate([at[:, hs], rt[:, hs]], axis=0),
                                jnp.concatenate([bt[:, hs], kt[:, hs]], axis=0)))
            bhat_t.append(bhat[:, hs].T)
            khat_t.append(khat[:, hs].T)

    rng = range(nchunk * RWKV_HEADS)
    pw = [jnp.where(strict, gram[i][0:CHUNK, 0:CHUNK], 0.0) for i in rng]
    a_ak = [jnp.where(strict, gram[i][0:CHUNK, CHUNK:2 * CHUNK], 0.0) for i in rng]
    a_rb = [jnp.where(incl, gram[i][CHUNK:2 * CHUNK, 0:CHUNK], 0.0) for i in rng]
    a_rk = [jnp.where(incl, gram[i][CHUNK:2 * CHUNK, CHUNK:2 * CHUNK], 0.0) for i in rng]
    akv = [_dot(a_ak[i], v_i[i]) for i in rng]
    rkv = [_dot(a_rk[i], v_i[i]) for i in rng]
    kv = [_dot(khat_t[i], v_i[i]) for i in rng]
    uu = [jnp.concatenate([at_i[i], akv[i]], axis=1) for i in rng]
    uu = [uu[i] + _dot(pw[i], uu[i]) for i in rng]
    for _ in range(5):
        pw = [_dot(pw[i], pw[i]) for i in rng]
        uu = [uu[i] + _dot(pw[i], uu[i]) for i in rng]
    ro = [_dot(a_rb[i], uu[i]) for i in rng]
    ms = [_dot(bhat_t[i], uu[i]) for i in rng]
    ra = [rt_i[i] + ro[i][:, 0:hd] for i in rng]
    ov = [ro[i][:, hd:2 * hd] + rkv[i] for i in rng]
    m_i = [ms[i][:, 0:hd] + jnp.where(eye, wtot_i[i], 0.0) for i in rng]
    sv = [ms[i][:, hd:2 * hd] + kv[i] for i in rng]

    s_all = s_ref[...]
    s_h = [s_all[:, h * hd:(h + 1) * hd] for h in range(RWKV_HEADS)]
    for c in range(nchunk):
        sl = slice(c * CHUNK, (c + 1) * CHUNK)
        base = c * RWKV_HEADS
        outs = [_dot3(ra[base + h], s_h[h]) + ov[base + h] for h in range(RWKV_HEADS)]
        s_h = [_dot3(m_i[base + h], s_h[h]) + sv[base + h] for h in range(RWKV_HEADS)]
        o = jnp.concatenate(outs, axis=1)
        mean = _group_sum(o, ones_bd) * (1.0 / hd)
        dev = o - mean
        var = _group_sum(dev * dev, ones_bd) * (1.0 / hd)
        on = dev * lax.rsqrt(var + RWKV_GN_EPS) * gnw_ref[...] + gnb_ref[...] + bonus[sl]
        o_ref[0, sl, :] = (on * g[sl]).astype(BF16)
    s_ref[...] = jnp.concatenate(s_h, axis=1)


def _rwkv(p_rwkv, mu, w0, lora_bd, a0, g2, k_k, k_a, r_k, gn_w, gn_b, ones_bd, tri, seq):
    bsz, t_pad, _ = p_rwkv.shape
    nt = t_pad // MIX_BLOCK
    c2 = lambda b, t: (0, 0)
    vec = pl.BlockSpec((1, D_RWKV), c2)
    return pl.pallas_call(
        _rwkv_body,
        grid=(bsz, nt),
        in_specs=[
            pl.BlockSpec((1, MIX_BLOCK, D_RWKV_IN), lambda b, t: (b, t, 0)),
            pl.BlockSpec((1, D_RWKV_IN), c2), vec,
            pl.BlockSpec((128, 2 * D_RWKV), c2), vec,
            pl.BlockSpec((D_GATE_LORA, D_RWKV), c2), vec, vec, vec, vec, vec,
            pl.BlockSpec((D_RWKV, D_RWKV), c2),
            pl.BlockSpec((CHUNK, CHUNK), c2),
        ],
        out_specs=pl.BlockSpec((1, MIX_BLOCK, D_RWKV), lambda b, t: (b, jnp.maximum(t - 1, 0), 0)),
        out_shape=jax.ShapeDtypeStruct((bsz, seq, D_RWKV), BF16),
        scratch_shapes=[pltpu.VMEM((1, D_RWKV_IN), F32), pltpu.VMEM((RWKV_HEAD_DIM, D_RWKV), F32)],
        compiler_params=pltpu.CompilerParams(
            dimension_semantics=("arbitrary", "arbitrary"), vmem_limit_bytes=VMEM_LIMIT),
        name="rwkv",
    )(p_rwkv, mu, w0, lora_bd, a0, g2, k_k, k_a, r_k, gn_w, gn_b, ones_bd, tri)


_CAND_COUNTS = tuple(PEER_TOPK // (a + 1) for a in range(PEER_TOPK))
_CAND_ROWS = -(-sum(_CAND_COUNTS) // 8) * 8


def _post_body(yl_ref, yr_ref, x_ref, woa_ref, wob_ref, g2_ref, wq_ref, qg_ref, k1_ref, k2_ref,
               h1_ref, xt_ref, l_ref, e1_ref, r2_ref, e2_ref, q_s, v1_s, v2_s, cand_s):
    tm = POST_ROWS
    h1 = (x_ref[...] + jnp.dot(yl_ref[...], woa_ref[...], preferred_element_type=F32)
          + jnp.dot(yr_ref[...], wob_ref[...], preferred_element_type=F32))
    h1_ref[...] = h1
    ms = jnp.mean(h1 * h1, axis=-1, keepdims=True)
    xn = (h1 * lax.rsqrt(ms + NORM_EPS)) * g2_ref[...]
    xt_ref[...] = xn.T.astype(BF16)
    q_s[...] = jnp.dot(xn.astype(BF16), wq_ref[...], preferred_element_type=F32)
    cand_s[...] = jnp.full(cand_s.shape, NEG_INF, F32)

    def head(h, carry):
        off = pl.multiple_of(h * PEER_D_QUERY, PEER_D_QUERY)
        qh = q_s[:, pl.ds(off, PEER_D_QUERY)]
        msq = jnp.mean(qh * qh, axis=-1, keepdims=True)
        qn = (qh * lax.rsqrt(msq + NORM_EPS)) * qg_ref[:, pl.ds(off, PEER_D_QUERY)]
        s1 = _dot3_nt(k1_ref[h], qn[:, 0:PEER_HALF])
        s2 = _dot3_nt(k2_ref[h], qn[:, PEER_HALF:PEER_D_QUERY])
        w1, w2 = s1, s2
        rank2 = jnp.full(s2.shape, float(PEER_TOPK), F32)
        for a in range(PEER_TOPK):
            m1 = jnp.max(w1, axis=0, keepdims=True)
            m2 = jnp.max(w2, axis=0, keepdims=True)
            v1_s[a:a + 1, :] = m1
            v2_s[a:a + 1, :] = m2
            hit2 = w2 == m2
            w1 = jnp.where(w1 == m1, NEG_INF, w1)
            w2 = jnp.where(hit2, NEG_INF, w2)
            rank2 = jnp.where(hit2, float(a), rank2)
        row = 0
        for a, nb in enumerate(_CAND_COUNTS):
            cand_s[row:row + nb, :] = v1_s[a:a + 1, :] + v2_s[0:nb, :]
            row += nb
        cand = cand_s[...]
        m0 = jnp.max(cand, axis=0, keepdims=True)
        m = m0
        z = jnp.zeros((1, tm), F32)
        for it in range(PEER_TOPK):
            if it:
                m = jnp.max(cand, axis=0, keepdims=True)
            z = z + jnp.exp(m - m0)
            cand = jnp.where(cand == m, NEG_INF, cand)
        tau = m
        cnt = jnp.zeros(s1.shape, F32)
        for b in range(PEER_TOPK):
            cnt = cnt + jnp.where(s1 + v2_s[b:b + 1, :] >= tau, 1.0, 0.0)
        l_ref[h] = cnt
        e1_ref[h] = jnp.exp(s1 - v1_s[0:1, :]) * (1.0 / z)
        r2_ref[h] = pltpu.bitcast(rank2.astype(BF16), U32)
        e2_ref[h] = pltpu.bitcast(jnp.exp(s2 - v2_s[0:1, :]).astype(BF16), U32)
        return carry

    lax.fori_loop(0, PEER_HEADS, head, 0)


def _post(y_lru, y_rwkv, x, wo_a, wo_b, g2, wq, qg, k1, k2):
    n = x.shape[0]
    tm = POST_ROWS
    c2 = lambda i: (0, 0)
    c3 = lambda i: (0, 0, 0)
    hk = (PEER_HEADS, PEER_N_KEYS)
    tok = pl.BlockSpec(hk + (tm,), lambda i: (0, 0, i))
    tok_packed = pl.BlockSpec((PEER_HEADS, PEER_N_KEYS // 2, tm), lambda i: (0, 0, i))
    return pl.pallas_call(
        _post_body,
        grid=(n // tm,),
        in_specs=[
            pl.BlockSpec((tm, D_LRU), lambda i: (i, 0)),
            pl.BlockSpec((tm, D_RWKV), lambda i: (i, 0)),
            pl.BlockSpec((tm, D_MODEL), lambda i: (i, 0)),
            pl.BlockSpec((D_LRU, D_MODEL), c2),
            pl.BlockSpec((D_RWKV, D_MODEL), c2),
            pl.BlockSpec((1, D_MODEL), c2),
            pl.BlockSpec((D_MODEL, PEER_HEADS * PEER_D_QUERY), c2),
            pl.BlockSpec((1, PEER_HEADS * PEER_D_QUERY), c2),
            pl.BlockSpec(hk + (PEER_HALF,), c3),
            pl.BlockSpec(hk + (PEER_HALF,), c3),
        ],
        out_specs=[
            pl.BlockSpec((tm, D_MODEL), lambda i: (i, 0)),
            pl.BlockSpec((D_MODEL, tm), lambda i: (0, i)),
            tok, tok, tok_packed, tok_packed,
        ],
        out_shape=[
            jax.ShapeDtypeStruct((n, D_MODEL), F32),
            jax.ShapeDtypeStruct((D_MODEL, n), BF16),
            jax.ShapeDtypeStruct(hk + (n,), F32),
            jax.ShapeDtypeStruct(hk + (n,), F32),
            jax.ShapeDtypeStruct((PEER_HEADS, PEER_N_KEYS // 2, n), U32),
            jax.ShapeDtypeStruct((PEER_HEADS, PEER_N_KEYS // 2, n), U32),
        ],
        scratch_shapes=[
            pltpu.VMEM((tm, PEER_HEADS * PEER_D_QUERY), F32),
            pltpu.VMEM((PEER_TOPK, tm), F32),
            pltpu.VMEM((PEER_TOPK, tm), F32),
            pltpu.VMEM((_CAND_ROWS, tm), F32),
        ],
        compiler_params=pltpu.CompilerParams(
            dimension_semantics=("arbitrary",), vmem_limit_bytes=VMEM_LIMIT),
        name="post",
    )(y_lru, y_rwkv, x, wo_a, wo_b, g2, wq, qg, k1, k2)


def _row_bcast_bf16(rows8, ii):
    packed = jnp.broadcast_to(rows8[ii:ii + 1, :], (16, LANE)).astype(BF16)
    return jnp.tile(packed, (PEER_N_KEYS // 16, 1))


def _experts_body(xt_ref, l_ref, e1_ref, r2_ref, e2_ref, u_ref, vt_ref, h1_ref, fg_ref, o_ref,
                  acc_ref, ht_ref, pt_ref):
    e = pl.program_id(1)
    tn = EXP_TOKENS
    half = tn // 2

    @pl.when(e == 0)
    def _():
        acc_ref[...] = jnp.zeros_like(acc_ref)

    i0 = pl.multiple_of(e * EXP_I, EXP_I)
    u = u_ref[...]
    for hf in range(2):
        hl = slice(hf * half, (hf + 1) * half)
        ht_ref[:, hl] = jnp.dot(u, xt_ref[:, hl], preferred_element_type=F32)

    for hf in range(2):
        for lb in range(hf * half // LANE, (hf + 1) * half // LANE):
            lanes = slice(lb * LANE, (lb + 1) * LANE)
            l8 = [l_ref[h, pl.ds(i0, EXP_I), lanes] for h in range(PEER_HEADS)]
            e18 = [e1_ref[h, pl.ds(i0, EXP_I), lanes] for h in range(PEER_HEADS)]
            for ii in range(EXP_I):
                rows = slice(ii * PEER_N_KEYS, (ii + 1) * PEER_N_KEYS)
                rows_packed = slice(ii * PEER_N_KEYS // 2, (ii + 1) * PEER_N_KEYS // 2)
                gate = jnp.zeros((PEER_N_KEYS, LANE), BF16)
                for h in range(PEER_HEADS):
                    sel = pltpu.bitcast(r2_ref[h, :, lanes], BF16) < _row_bcast_bf16(l8[h], ii)
                    val = pltpu.bitcast(e2_ref[h, :, lanes], BF16) * _row_bcast_bf16(e18[h], ii)
                    gate = gate + jnp.where(sel, val, jnp.zeros_like(val))
                pt = gate * _gelu_erf(ht_ref[rows, lanes]).astype(BF16)
                pt_ref[rows_packed, lanes] = pltpu.bitcast(pt, U32)
        hl = slice(hf * half, (hf + 1) * half)
        acc_ref[:, hl] += jnp.dot(vt_ref[...], pltpu.bitcast(pt_ref[:, hl], BF16),
                                  preferred_element_type=F32)

    @pl.when(e == pl.num_programs(1) - 1)
    def _():
        h2 = h1_ref[...] + acc_ref[...].T
        ms = jnp.mean(h2 * h2, axis=-1, keepdims=True)
        o_ref[...] = (h2 * lax.rsqrt(ms + NORM_EPS)) * fg_ref[...]


def _experts(xt, lcnt, e1, r2, e2, u_bf, vt_bf, h1, fgain):
    n = h1.shape[0]
    tn = EXP_TOKENS
    te = EXP_I * PEER_N_KEYS
    hk = (PEER_HEADS, PEER_N_KEYS)
    tok = pl.BlockSpec(hk + (tn,), lambda i, e: (0, 0, i))
    tok_packed = pl.BlockSpec((PEER_HEADS, PEER_N_KEYS // 2, tn), lambda i, e: (0, 0, i))
    return pl.pallas_call(
        _experts_body,
        grid=(n // tn, N_EXPERTS // te),
        in_specs=[
            pl.BlockSpec((D_MODEL, tn), lambda i, e: (0, i)),
            tok, tok, tok_packed, tok_packed,
            pl.BlockSpec((te, D_MODEL), lambda i, e: (e, 0)),
            pl.BlockSpec((D_MODEL, te), lambda i, e: (0, e)),
            pl.BlockSpec((tn, D_MODEL), lambda i, e: (i, 0)),
            pl.BlockSpec((1, D_MODEL), lambda i, e: (0, 0)),
        ],
        out_specs=pl.BlockSpec((tn, D_MODEL), lambda i, e: (i, 0)),
        out_shape=jax.ShapeDtypeStruct((n, D_MODEL), F32),
        scratch_shapes=[
            pltpu.VMEM((D_MODEL, tn), F32),
            pltpu.VMEM((te, tn), F32),
            pltpu.VMEM((te // 2, tn), U32),
        ],
        compiler_params=pltpu.CompilerParams(
            dimension_semantics=("arbitrary", "arbitrary"), vmem_limit_bytes=VMEM_LIMIT),
        name="experts",
    )(xt, lcnt, e1, r2, e2, u_bf, vt_bf, h1, fgain)


def _block_diag(w):
    nh, d, _ = w.shape
    eye = jnp.eye(nh, dtype=w.dtype)
    return (eye[:, None, :, None] * w[:, :, None, :]).reshape(nh * d, nh * d)


def kernel(x, meta_tokens, norm1_gain, w_in, conv_w, conv_b, lru_gate_a_w, lru_gate_a_b, lru_gate_x_w, lru_gate_x_b, lru_lambda, lru_out_gain, rwkv_shift_mu, rwkv_w0, rwkv_w2, rwkv_a0, rwkv_a2, rwkv_g2, rwkv_k_k, rwkv_k_a, rwkv_r_k, rwkv_gn_w, rwkv_gn_b, w_out, norm2_gain, peer_w_query, peer_q_gain, peer_sub_keys, peer_u, peer_v, final_norm_gain):
    bsz, seq, _ = x.shape
    row = lambda v: v.reshape(1, -1).astype(F32)

    w_in_bf = w_in[0].astype(BF16)
    w_lru, w_rwkv = w_in_bf[:, :2 * D_LRU], w_in_bf[:, 2 * D_LRU:]
    wa_bd = _block_diag(lru_gate_a_w[0]).astype(BF16)
    wx_bd = _block_diag(lru_gate_x_w[0]).astype(BF16)
    zeros_l = jnp.zeros((D_DECAY_LORA, D_RWKV), F32)
    lora_bd = jnp.concatenate(
        [jnp.concatenate([rwkv_w2[0], zeros_l], axis=1),
         jnp.concatenate([zeros_l, rwkv_a2[0]], axis=1)], axis=0).astype(BF16)
    ones_bd = _block_diag(jnp.ones((RWKV_HEADS, RWKV_HEAD_DIM, RWKV_HEAD_DIM), F32)).astype(BF16)
    tri = jnp.tril(jnp.ones((CHUNK, CHUNK), F32)).astype(BF16)
    wo_bf = w_out[0].astype(BF16)
    wq_bf = peer_w_query[0].astype(BF16)
    u_bf = peer_u[0].astype(BF16)
    vt_bf = peer_v[0].T.astype(BF16)

    head = jnp.concatenate(
        [jnp.zeros((FRONT_PAD, D_MODEL), x.dtype), meta_tokens.astype(x.dtype)], axis=0)
    h0 = jnp.concatenate([jnp.broadcast_to(head[None], (bsz, MIX_BLOCK, D_MODEL)), x], axis=1)
    t_pad = seq + MIX_BLOCK
    p_lru, p_rwkv = _inproj(h0.reshape(bsz * t_pad, D_MODEL), row(norm1_gain[0]), w_lru, w_rwkv)
    y_lru = _lru(p_lru.reshape(bsz, t_pad, 2 * D_LRU), conv_w[0], row(conv_b[0]), wa_bd,
                 row(lru_gate_a_b[0]), wx_bd, row(lru_gate_x_b[0]), row(lru_lambda[0]),
                 row(lru_out_gain[0]), seq)
    y_rwkv = _rwkv(p_rwkv.reshape(bsz, t_pad, D_RWKV_IN), row(rwkv_shift_mu[0]), row(rwkv_w0[0]),
                   lora_bd, row(rwkv_a0[0]), rwkv_g2[0].astype(BF16), row(rwkv_k_k[0]),
                   row(rwkv_k_a[0]), row(rwkv_r_k[0]), row(rwkv_gn_w[0]), row(rwkv_gn_b[0]),
                   ones_bd, tri, seq)

    n = bsz * seq
    h1, xt, lcnt, e1, r2, e2 = _post(
        y_lru.reshape(n, D_LRU), y_rwkv.reshape(n, D_RWKV), x.reshape(n, D_MODEL),
        wo_bf[:D_LRU], wo_bf[D_LRU:], row(norm2_gain[0]), wq_bf, row(peer_q_gain[0]),
        peer_sub_keys[0, :, 0], peer_sub_keys[0, :, 1])
    out = _experts(xt, lcnt, e1, r2, e2, u_bf, vt_bf, h1, row(final_norm_gain))
    return out.reshape(bsz, seq, D_MODEL)
```

```python
import functools

import jax
import jax.numpy as jnp
from jax import lax
from jax.experimental import pallas as pl
from jax.experimental.pallas import tpu as pltpu

F32 = jnp.float32
BF16 = jnp.bfloat16
U32 = jnp.uint32

D_MODEL = 1024
N_META = 16
NORM_EPS = 1e-6
D_LRU = 512
D_RWKV = 512
LRU_HEADS = 8
LRU_HEAD_DIM = D_LRU // LRU_HEADS
LRU_C = 8.0
RWKV_HEAD_DIM = 64
RWKV_HEADS = D_RWKV // RWKV_HEAD_DIM
D_DECAY_LORA = 64
D_AAA_LORA = 64
D_GATE_LORA = 128
RWKV_GN_EPS = 64e-5
D_RWKV_IN = 3 * D_RWKV + D_DECAY_LORA + D_AAA_LORA + D_GATE_LORA
PEER_HEADS = 8
PEER_N_KEYS = 128
PEER_D_QUERY = 256
PEER_HALF = PEER_D_QUERY // 2
PEER_TOPK = 16
N_EXPERTS = PEER_N_KEYS * PEER_N_KEYS

MIX_BLOCK = 128
FRONT_PAD = MIX_BLOCK - N_META
CHUNK = 64
INPROJ_ROWS = 256
POST_ROWS = 256
EXP_TOKENS = 512
EXP_I = 8
LANE = 128
NEG_INF = float("-inf")
VMEM_LIMIT = 52 * 1024 * 1024


def _dot(a, b):
    return jnp.dot(a.astype(BF16), b.astype(BF16), preferred_element_type=F32)


def _dot_nt(a, b):
    return lax.dot_general(a.astype(BF16), b.astype(BF16), (((1,), (1,)), ((), ())),
                           preferred_element_type=F32)


def _dot_tn(a, b):
    return lax.dot_general(a.astype(BF16), b.astype(BF16), (((0,), (0,)), ((), ())),
                           preferred_element_type=F32)


def _split2(x):
    hi = x.astype(BF16)
    lo = (x - hi.astype(F32)).astype(BF16)
    return hi, lo


def _split3(x):
    hi = x.astype(BF16)
    r1 = x - hi.astype(F32)
    mid = r1.astype(BF16)
    lo = (r1 - mid.astype(F32)).astype(BF16)
    return hi, mid, lo


def _dot3(a, b):
    ah, al = _split2(a)
    bh, bl = _split2(b)
    return (jnp.dot(ah, bh, preferred_element_type=F32)
            + jnp.dot(al, bh, preferred_element_type=F32)
            + jnp.dot(ah, bl, preferred_element_type=F32))


def _dot3_nt(a, b):
    ah, al = _split2(a)
    bh, bl = _split2(b)
    dn = (((1,), (1,)), ((), ()))
    return (lax.dot_general(ah, bh, dn, preferred_element_type=F32)
            + lax.dot_general(al, bh, dn, preferred_element_type=F32)
            + lax.dot_general(ah, bl, dn, preferred_element_type=F32))


def _group_sum(x, ones_bd):
    hi, lo = _split2(x)
    return (jnp.dot(hi, ones_bd, preferred_element_type=F32)
            + jnp.dot(lo, ones_bd, preferred_element_type=F32))


def _softplus(x):
    return jnp.maximum(x, 0.0) + jnp.log1p(jnp.exp(-jnp.abs(x)))


def _sigmoid(x):
    return 1.0 / (1.0 + jnp.exp(-x))


def _gelu_tanh(x):
    return 0.5 * x * (1.0 + jnp.tanh(0.7978845608028654 * (x + 0.044715 * (x * x * x))))


def _gelu_erf(x):
    return 0.5 * x * (1.0 + lax.erf(x * 0.7071067811865476))


def _inproj_body(h_ref, g_ref, wl_ref, wr_ref, ol_ref, or_ref):
    x = h_ref[...]
    ms = jnp.mean(x * x, axis=-1, keepdims=True)
    xn = ((x * lax.rsqrt(ms + NORM_EPS)) * g_ref[...]).astype(BF16)
    ol_ref[...] = jnp.dot(xn, wl_ref[...], preferred_element_type=F32)
    or_ref[...] = jnp.dot(xn, wr_ref[...], preferred_element_type=F32)


def _inproj(h, gain, w_lru, w_rwkv):
    n = h.shape[0]
    tm = INPROJ_ROWS
    const = lambda i: (0, 0)
    return pl.pallas_call(
        _inproj_body,
        grid=(n // tm,),
        in_specs=[
            pl.BlockSpec((tm, D_MODEL), lambda i: (i, 0)),
            pl.BlockSpec((1, D_MODEL), const),
            pl.BlockSpec((D_MODEL, 2 * D_LRU), const),
            pl.BlockSpec((D_MODEL, D_RWKV_IN), const),
        ],
        out_specs=[
            pl.BlockSpec((tm, 2 * D_LRU), lambda i: (i, 0)),
            pl.BlockSpec((tm, D_RWKV_IN), lambda i: (i, 0)),
        ],
        out_shape=[
            jax.ShapeDtypeStruct((n, 2 * D_LRU), F32),
            jax.ShapeDtypeStruct((n, D_RWKV_IN), F32),
        ],
        compiler_params=pltpu.CompilerParams(
            dimension_semantics=("arbitrary",), vmem_limit_bytes=VMEM_LIMIT),
        name="inproj",
    )(h, gain, w_lru, w_rwkv)


def _lru_body(p_ref, cw_ref, cb_ref, wa_ref, ba_ref, wx_ref, bx_ref, lam_ref, gain_ref,
              o_ref, xext_ref, h_ref):
    t = pl.program_id(1)
    tb = MIX_BLOCK

    @pl.when(t == 0)
    def _():
        xext_ref[0:8, :] = jnp.zeros((8, D_LRU), F32)
        h_ref[...] = jnp.zeros_like(h_ref)

    x = p_ref[0, :, 0:D_LRU]
    gate = p_ref[0, :, D_LRU:2 * D_LRU]
    xext_ref[8:8 + tb, :] = x
    cw = cw_ref[...]
    xc = (cb_ref[...] + cw[3:4, :] * x
          + cw[2:3, :] * xext_ref[7:7 + tb, :]
          + cw[1:2, :] * xext_ref[6:6 + tb, :]
          + cw[0:1, :] * xext_ref[5:5 + tb, :])
    xext_ref[0:8, :] = x[tb - 8:tb, :]

    xcb = xc.astype(BF16)
    r = _sigmoid(jnp.dot(xcb, wa_ref[...], preferred_element_type=F32) + ba_ref[...])
    i = _sigmoid(jnp.dot(xcb, wx_ref[...], preferred_element_type=F32) + bx_ref[...])
    log_a = (-LRU_C) * r * _softplus(-lam_ref[...])
    a = jnp.exp(log_a)
    th = jnp.tanh(log_a)
    one_minus_a2 = (-2.0 * th) / (1.0 - th)
    b = jnp.sqrt(one_minus_a2) * (i * xc)
    rows = lax.broadcasted_iota(jnp.int32, (tb, D_LRU), 0)
    b = jnp.where(rows + t * tb >= FRONT_PAD, b, 0.0)

    d = 1
    while d < tb:
        keep = rows >= d
        a_sh = jnp.where(keep, pltpu.roll(a, d, axis=0), 1.0)
        b_sh = jnp.where(keep, pltpu.roll(b, d, axis=0), 0.0)
        b = a * b_sh + b
        a = a * a_sh
        d *= 2
    h = a * h_ref[...] + b
    h_ref[...] = h[tb - 1:tb, :]

    y = h * _gelu_tanh(gate)
    ms = jnp.mean(y * y, axis=-1, keepdims=True)
    o_ref[0] = ((y * lax.rsqrt(ms + NORM_EPS)) * gain_ref[...]).astype(BF16)


def _lru(p_lru, conv_w, conv_b, wa_bd, ba, wx_bd, bx, lam, gain, seq):
    bsz, t_pad, _ = p_lru.shape
    nt = t_pad // MIX_BLOCK
    c2 = lambda b, t: (0, 0)
    vec = pl.BlockSpec((1, D_LRU), c2)
    mat = pl.BlockSpec((D_LRU, D_LRU), c2)
    return pl.pallas_call(
        _lru_body,
        grid=(bsz, nt),
        in_specs=[
            pl.BlockSpec((1, MIX_BLOCK, 2 * D_LRU), lambda b, t: (b, t, 0)),
            pl.BlockSpec((4, D_LRU), c2), vec, mat, vec, mat, vec, vec, vec,
        ],
        out_specs=pl.BlockSpec((1, MIX_BLOCK, D_LRU), lambda b, t: (b, jnp.maximum(t - 1, 0), 0)),
        out_shape=jax.ShapeDtypeStruct((bsz, seq, D_LRU), BF16),
        scratch_shapes=[pltpu.VMEM((MIX_BLOCK + 8, D_LRU), F32), pltpu.VMEM((1, D_LRU), F32)],
        compiler_params=pltpu.CompilerParams(
            dimension_semantics=("arbitrary", "arbitrary"), vmem_limit_bytes=VMEM_LIMIT),
        name="lru",
    )(p_lru, conv_w, conv_b, wa_bd, ba, wx_bd, bx, lam, gain)


def _rwkv_body(p_ref, mu_ref, w0_ref, lora_ref, a0_ref, g2_ref, kk_ref, ka_ref, rk_ref,
               gnw_ref, gnb_ref, ones_ref, tri_ref, o_ref, carry_ref, s_ref):
    t = pl.program_id(1)
    tb = MIX_BLOCK
    hd = RWKV_HEAD_DIM

    @pl.when(t == 0)
    def _():
        carry_ref[...] = jnp.zeros_like(carry_ref)
        s_ref[...] = jnp.zeros_like(s_ref)

    p = p_ref[0]
    rows = lax.broadcasted_iota(jnp.int32, (tb, D_RWKV_IN), 0)
    prev = jnp.where(rows == 0, carry_ref[...], pltpu.roll(p, 1, axis=0))
    carry_ref[...] = p[tb - 1:tb, :]
    ps = p + (prev - p) * mu_ref[...]

    r = ps[:, 0:D_RWKV]
    k = ps[:, D_RWKV:2 * D_RWKV]
    v = ps[:, 2 * D_RWKV:3 * D_RWKV]
    lo = ps[:, 3 * D_RWKV:3 * D_RWKV + 128]
    gl = ps[:, 3 * D_RWKV + 128:D_RWKV_IN]
    lane = lax.broadcasted_iota(jnp.int32, (tb, 128), 1)
    lo_act = jnp.where(lane < D_DECAY_LORA, jnp.tanh(lo), lo)
    la = _dot(lo_act, lora_ref[...])
    log_w = -_softplus(-(w0_ref[...] + la[:, 0:D_RWKV])) - 0.5
    ld = -jnp.exp(log_w)
    a = _sigmoid(a0_ref[...] + la[:, D_RWKV:2 * D_RWKV])
    g = _dot(_sigmoid(gl), g2_ref[...])

    ones_bd = ones_ref[...]
    kkr = k * kk_ref[...]
    kk = kkr / jnp.maximum(jnp.sqrt(_group_sum(kkr * kkr, ones_bd)), 1e-12)
    k2 = k * (1.0 + (a - 1.0) * ka_ref[...])
    bonus = _group_sum(r * k2 * rk_ref[...], ones_bd) * v
    kka = kk * a

    tri = tri_ref[...]
    ri = lax.broadcasted_iota(jnp.int32, (CHUNK, CHUNK), 0)
    ci = lax.broadcasted_iota(jnp.int32, (CHUNK, CHUNK), 1)
    strict = ri > ci
    incl = ri >= ci
    eye = ri == ci

    nchunk = tb // CHUNK
    at_i, rt_i, v_i, gram, bhat_t, khat_t, wtot_i = [], [], [], [], [], [], []
    for c in range(nchunk):
        sl = slice(c * CHUNK, (c + 1) * CHUNK)
        ld_c = ld[sl]
        hi, mid, lw = _split3(ld_c)
        cum = (jnp.dot(tri, hi, preferred_element_type=F32)
               + jnp.dot(tri, mid, preferred_element_type=F32)
               + jnp.dot(tri, lw, preferred_element_type=F32))
        ltot = cum[CHUNK - 1:CHUNK, :]
        e_inv = jnp.exp(-cum)
        e_end = jnp.exp(ltot - cum)
        at = -kk[sl] * jnp.exp(cum - ld_c)
        rt = r[sl] * jnp.exp(cum)
        bt = kka[sl] * e_inv
        kt = k2[sl] * e_inv
        bhat = kka[sl] * e_end
        khat = k2[sl] * e_end
        wtot = jnp.exp(ltot)
        for h in range(RWKV_HEADS):
            hs = slice(h * hd, (h + 1) * hd)
            at_i.append(at[:, hs])
            rt_i.append(rt[:, hs])
            v_i.append(v[sl, hs])
            wtot_i.append(wtot[:, hs])
            gram.append(_dot_nt(jnp.concatenate([at[:, hs], rt[:, hs]], axis=0),
                                jnp.concatenate([bt[:, hs], kt[:, hs]], axis=0)))
            bhat_t.append(bhat[:, hs].T)
            khat_t.append(khat[:, hs].T)

    rng = range(nchunk * RWKV_HEADS)
    pw = [jnp.where(strict, gram[i][0:CHUNK, 0:CHUNK], 0.0) for i in rng]
    a_ak = [jnp.where(strict, gram[i][0:CHUNK, CHUNK:2 * CHUNK], 0.0) for i in rng]
    a_rb = [jnp.where(incl, gram[i][CHUNK:2 * CHUNK, 0:CHUNK], 0.0) for i in rng]
    a_rk = [jnp.where(incl, gram[i][CHUNK:2 * CHUNK, CHUNK:2 * CHUNK], 0.0) for i in rng]
    akv = [_dot(a_ak[i], v_i[i]) for i in rng]
    rkv = [_dot(a_rk[i], v_i[i]) for i in rng]
    kv = [_dot(khat_t[i], v_i[i]) for i in rng]
    uu = [jnp.concatenate([at_i[i], akv[i]], axis=1) for i in rng]
    uu = [uu[i] + _dot(pw[i], uu[i]) for i in rng]
    for _ in range(5):
        pw = [_dot(pw[i], pw[i]) for i in rng]
        uu = [uu[i] + _dot(pw[i], uu[i]) for i in rng]
    ro = [_dot(a_rb[i], uu[i]) for i in rng]
    ms = [_dot(bhat_t[i], uu[i]) for i in rng]
    ra = [rt_i[i] + ro[i][:, 0:hd] for i in rng]
    ov = [ro[i][:, hd:2 * hd] + rkv[i] for i in rng]
    m_i = [ms[i][:, 0:hd] + jnp.where(eye, wtot_i[i], 0.0) for i in rng]
    sv = [ms[i][:, hd:2 * hd] + kv[i] for i in rng]

    s_all = s_ref[...]
    s_h = [s_all[:, h * hd:(h + 1) * hd] for h in range(RWKV_HEADS)]
    for c in range(nchunk):
        sl = slice(c * CHUNK, (c + 1) * CHUNK)
        base = c * RWKV_HEADS
        outs = [_dot3(ra[base + h], s_h[h]) + ov[base + h] for h in range(RWKV_HEADS)]
        s_h = [_dot3(m_i[base + h], s_h[h]) + sv[base + h] for h in range(RWKV_HEADS)]
        o = jnp.concatenate(outs, axis=1)
        mean = _group_sum(o, ones_bd) * (1.0 / hd)
        dev = o - mean
        var = _group_sum(dev * dev, ones_bd) * (1.0 / hd)
        on = dev * lax.rsqrt(var + RWKV_GN_EPS) * gnw_ref[...] + gnb_ref[...] + bonus[sl]
        o_ref[0, sl, :] = (on * g[sl]).astype(BF16)
    s_ref[...] = jnp.concatenate(s_h, axis=1)


def _rwkv(p_rwkv, mu, w0, lora_bd, a0, g2, k_k, k_a, r_k, gn_w, gn_b, ones_bd, tri, seq):
    bsz, t_pad, _ = p_rwkv.shape
    nt = t_pad // MIX_BLOCK
    c2 = lambda b, t: (0, 0)
    vec = pl.BlockSpec((1, D_RWKV), c2)
    return pl.pallas_call(
        _rwkv_body,
        grid=(bsz, nt),
        in_specs=[
            pl.BlockSpec((1, MIX_BLOCK, D_RWKV_IN), lambda b, t: (b, t, 0)),
            pl.BlockSpec((1, D_RWKV_IN), c2), vec,
            pl.BlockSpec((128, 2 * D_RWKV), c2), vec,
            pl.BlockSpec((D_GATE_LORA, D_RWKV), c2), vec, vec, vec, vec, vec,
            pl.BlockSpec((D_RWKV, D_RWKV), c2),
            pl.BlockSpec((CHUNK, CHUNK), c2),
        ],
        out_specs=pl.BlockSpec((1, MIX_BLOCK, D_RWKV), lambda b, t: (b, jnp.maximum(t - 1, 0), 0)),
        out_shape=jax.ShapeDtypeStruct((bsz, seq, D_RWKV), BF16),
        scratch_shapes=[pltpu.VMEM((1, D_RWKV_IN), F32), pltpu.VMEM((RWKV_HEAD_DIM, D_RWKV), F32)],
        compiler_params=pltpu.CompilerParams(
            dimension_semantics=("arbitrary", "arbitrary"), vmem_limit_bytes=VMEM_LIMIT),
        name="rwkv",
    )(p_rwkv, mu, w0, lora_bd, a0, g2, k_k, k_a, r_k, gn_w, gn_b, ones_bd, tri)


_CAND_COUNTS = tuple(PEER_TOPK // (a + 1) for a in range(PEER_TOPK))
_CAND_ROWS = -(-sum(_CAND_COUNTS) // 8) * 8


def _post_body(yl_ref, yr_ref, x_ref, woa_ref, wob_ref, g2_ref, wq_ref, qg_ref, k1_ref, k2_ref,
               h1_ref, xt_ref, l_ref, e1_ref, r2_ref, e2_ref, q_s, v1_s, v2_s, cand_s):
    tm = POST_ROWS
    h1 = (x_ref[...] + jnp.dot(yl_ref[...], woa_ref[...], preferred_element_type=F32)
          + jnp.dot(yr_ref[...], wob_ref[...], preferred_element_type=F32))
    h1_ref[...] = h1
    ms = jnp.mean(h1 * h1, axis=-1, keepdims=True)
    xn = (h1 * lax.rsqrt(ms + NORM_EPS)) * g2_ref[...]
    xt_ref[...] = pltpu.bitcast(xn.T.astype(BF16), U32)
    q_s[...] = jnp.dot(xn.astype(BF16), wq_ref[...], preferred_element_type=F32)
    cand_s[...] = jnp.full(cand_s.shape, NEG_INF, F32)

    def head(h, carry):
        off = pl.multiple_of(h * PEER_D_QUERY, PEER_D_QUERY)
        qh = q_s[:, pl.ds(off, PEER_D_QUERY)]
        msq = jnp.mean(qh * qh, axis=-1, keepdims=True)
        qn = (qh * lax.rsqrt(msq + NORM_EPS)) * qg_ref[:, pl.ds(off, PEER_D_QUERY)]
        s1 = _dot3_nt(k1_ref[h], qn[:, 0:PEER_HALF])
        s2 = _dot3_nt(k2_ref[h], qn[:, PEER_HALF:PEER_D_QUERY])
        w1, w2 = s1, s2
        rank2 = jnp.full(s2.shape, float(PEER_TOPK), F32)
        for a in range(PEER_TOPK):
            m1 = jnp.max(w1, axis=0, keepdims=True)
            m2 = jnp.max(w2, axis=0, keepdims=True)
            v1_s[a:a + 1, :] = m1
            v2_s[a:a + 1, :] = m2
            hit2 = w2 == m2
            w1 = jnp.where(w1 == m1, NEG_INF, w1)
            w2 = jnp.where(hit2, NEG_INF, w2)
            rank2 = jnp.where(hit2, float(a), rank2)
        row = 0
        for a, nb in enumerate(_CAND_COUNTS):
            cand_s[row:row + nb, :] = v1_s[a:a + 1, :] + v2_s[0:nb, :]
            row += nb
        cand = cand_s[...]
        m0 = jnp.max(cand, axis=0, keepdims=True)
        m = m0
        z = jnp.zeros((1, tm), F32)
        for it in range(PEER_TOPK):
            if it:
                m = jnp.max(cand, axis=0, keepdims=True)
            z = z + jnp.exp(m - m0)
            cand = jnp.where(cand == m, NEG_INF, cand)
        tau = m
        cnt = jnp.zeros(s1.shape, F32)
        for b in range(PEER_TOPK):
            cnt = cnt + jnp.where(s1 + v2_s[b:b + 1, :] >= tau, 1.0, 0.0)
        l_ref[h] = cnt
        e1_ref[h] = jnp.exp(s1 - v1_s[0:1, :]) * (1.0 / z)
        r2_packed = pltpu.bitcast(rank2.astype(BF16), U32)
        e2_packed = pltpu.bitcast(jnp.exp(s2 - v2_s[0:1, :]).astype(BF16), U32)
        for lb in range(tm // LANE):
            r2_ref[h, lb] = r2_packed[:, lb * LANE:(lb + 1) * LANE]
            e2_ref[h, lb] = e2_packed[:, lb * LANE:(lb + 1) * LANE]
        return carry

    lax.fori_loop(0, PEER_HEADS, head, 0)


def _post(y_lru, y_rwkv, x, wo_a, wo_b, g2, wq, qg, k1, k2):
    n = x.shape[0]
    tm = POST_ROWS
    c2 = lambda i: (0, 0)
    c3 = lambda i: (0, 0, 0)
    hk = (PEER_HEADS, PEER_N_KEYS)
    tok = pl.BlockSpec(hk + (tm,), lambda i: (0, 0, i))
    tok_packed = pl.BlockSpec((PEER_HEADS, tm // LANE, PEER_N_KEYS // 2, LANE),
                              lambda i: (0, i, 0, 0))
    return pl.pallas_call(
        _post_body,
        grid=(n // tm,),
        in_specs=[
            pl.BlockSpec((tm, D_LRU), lambda i: (i, 0)),
            pl.BlockSpec((tm, D_RWKV), lambda i: (i, 0)),
            pl.BlockSpec((tm, D_MODEL), lambda i: (i, 0)),
            pl.BlockSpec((D_LRU, D_MODEL), c2),
            pl.BlockSpec((D_RWKV, D_MODEL), c2),
            pl.BlockSpec((1, D_MODEL), c2),
            pl.BlockSpec((D_MODEL, PEER_HEADS * PEER_D_QUERY), c2),
            pl.BlockSpec((1, PEER_HEADS * PEER_D_QUERY), c2),
            pl.BlockSpec(hk + (PEER_HALF,), c3),
            pl.BlockSpec(hk + (PEER_HALF,), c3),
        ],
        out_specs=[
            pl.BlockSpec((tm, D_MODEL), lambda i: (i, 0)),
            pl.BlockSpec((D_MODEL // 2, tm), lambda i: (0, i)),
            tok, tok, tok_packed, tok_packed,
        ],
        out_shape=[
            jax.ShapeDtypeStruct((n, D_MODEL), F32),
            jax.ShapeDtypeStruct((D_MODEL // 2, n), U32),
            jax.ShapeDtypeStruct(hk + (n,), F32),
            jax.ShapeDtypeStruct(hk + (n,), F32),
            jax.ShapeDtypeStruct((PEER_HEADS, n // LANE, PEER_N_KEYS // 2, LANE), U32),
            jax.ShapeDtypeStruct((PEER_HEADS, n // LANE, PEER_N_KEYS // 2, LANE), U32),
        ],
        scratch_shapes=[
            pltpu.VMEM((tm, PEER_HEADS * PEER_D_QUERY), F32),
            pltpu.VMEM((PEER_TOPK, tm), F32),
            pltpu.VMEM((PEER_TOPK, tm), F32),
            pltpu.VMEM((_CAND_ROWS, tm), F32),
        ],
        compiler_params=pltpu.CompilerParams(
            dimension_semantics=("arbitrary",), vmem_limit_bytes=VMEM_LIMIT),
        name="post",
    )(y_lru, y_rwkv, x, wo_a, wo_b, g2, wq, qg, k1, k2)


def _row_bcast_bf16(rows8, ii):
    packed = jnp.broadcast_to(rows8[ii:ii + 1, :], (16, LANE)).astype(BF16)
    return jnp.tile(packed, (PEER_N_KEYS // 16, 1))


EXP_TILES = N_EXPERTS // (EXP_I * PEER_N_KEYS)
EXP_LAG = 2
EXP_MROWS = 128


def _experts_body(xt_ref, l_ref, e1_ref, r2_ref, e2_ref, u_ref, vt_ref, h1_ref, fg_ref, o_ref,
                  acc_ref, ht0_ref, ht1_ref, pt0_ref, pt1_ref, lrow_ref, erow_ref):
    s = pl.program_id(0)
    tn = EXP_TOKENS
    half = tn // 2

    @pl.when(s == 0)
    def _():
        for ref in (acc_ref, ht0_ref, ht1_ref, pt0_ref, pt1_ref):
            ref[...] = jnp.zeros_like(ref)

    e_gate = (s + EXP_TILES - 1) % EXP_TILES
    e_out = (s + EXP_TILES - 2) % EXP_TILES
    i0 = pl.multiple_of(e_gate * EXP_I, EXP_I)

    for h in range(PEER_HEADS):
        lrow_ref[h] = l_ref[h, pl.ds(i0, EXP_I), :]
        erow_ref[h] = e1_ref[h, pl.ds(i0, EXP_I), :]

    def stage(ht_w, ht_r, pt_w, pt_r):
        def hidden_job(mi, hf):
            rs = slice(mi * EXP_MROWS, (mi + 1) * EXP_MROWS)
            hl = slice(hf * half, (hf + 1) * half)
            rp = slice(mi * EXP_MROWS // 2, (mi + 1) * EXP_MROWS // 2)
            ht_w[rs, hl] = jnp.dot(pltpu.bitcast(u_ref[rp, :], BF16),
                                   pltpu.bitcast(xt_ref[:, hl], BF16), preferred_element_type=F32)

        def output_job(mi, hf):
            rs = slice(mi * EXP_MROWS, (mi + 1) * EXP_MROWS)
            hl = slice(hf * half, (hf + 1) * half)
            rp = slice(mi * EXP_MROWS // 2, (mi + 1) * EXP_MROWS // 2)
            acc_ref[rs, hl] += jnp.dot(pltpu.bitcast(vt_ref[rp, :], BF16),
                                       pltpu.bitcast(pt_r[:, hl], BF16), preferred_element_type=F32)

        def gate_piece(lb, ii):
            lanes = slice(lb * LANE, (lb + 1) * LANE)
            rows = slice(ii * PEER_N_KEYS, (ii + 1) * PEER_N_KEYS)
            rows_packed = slice(ii * PEER_N_KEYS // 2, (ii + 1) * PEER_N_KEYS // 2)
            gate = jnp.zeros((PEER_N_KEYS, LANE), BF16)
            for h in range(PEER_HEADS):
                lrow = _row_bcast_bf16(lrow_ref[h, :, lanes], ii)
                erow = _row_bcast_bf16(erow_ref[h, :, lanes], ii)
                sel = pltpu.bitcast(r2_ref[h, lb], BF16) < lrow
                val = pltpu.bitcast(e2_ref[h, lb], BF16) * erow
                gate = gate + jnp.where(sel, val, jnp.zeros_like(val))
            pt = gate * _gelu_erf(ht_r[rows, lanes]).astype(BF16)
            pt_w[rows_packed, lanes] = pltpu.bitcast(pt, U32)

        nm = (EXP_I * PEER_N_KEYS) // EXP_MROWS
        mxu_jobs = ([functools.partial(hidden_job, mi, hf) for hf in range(2) for mi in range(nm)]
                    + [functools.partial(output_job, mi, hf) for hf in range(2)
                       for mi in range(D_MODEL // EXP_MROWS)])
        pieces = [functools.partial(gate_piece, lb, ii)
                  for lb in range(tn // LANE) for ii in range(EXP_I)]
        per_job = -(-len(pieces) // len(mxu_jobs))
        for j, job in enumerate(mxu_jobs):
            job()
            for piece in pieces[j * per_job:(j + 1) * per_job]:
                piece()

    @pl.when(s % 2 == 0)
    def _():
        stage(ht0_ref, ht1_ref, pt0_ref, pt1_ref)

    @pl.when(s % 2 == 1)
    def _():
        stage(ht1_ref, ht0_ref, pt1_ref, pt0_ref)

    @pl.when(jnp.logical_and(e_out == EXP_TILES - 1, s >= EXP_LAG))
    def _():
        h2 = h1_ref[...] + acc_ref[...].T
        ms = jnp.mean(h2 * h2, axis=-1, keepdims=True)
        o_ref[...] = (h2 * lax.rsqrt(ms + NORM_EPS)) * fg_ref[...]
        acc_ref[...] = jnp.zeros_like(acc_ref)


def _experts(xt, lcnt, e1, r2, e2, u_packed, vt_packed, h1, fgain):
    n = h1.shape[0]
    tn = EXP_TOKENS
    te = EXP_I * PEER_N_KEYS
    nb = n // tn
    hk = (PEER_HEADS, PEER_N_KEYS)

    def tok_block(lag):
        return lambda s: jnp.clip((s - lag) // EXP_TILES, 0, nb - 1)

    def tile(lag):
        return lambda s: (s + EXP_TILES - lag) % EXP_TILES

    tb0, tb1, tb2 = tok_block(0), tok_block(1), tok_block(2)
    tok = pl.BlockSpec(hk + (tn,), lambda s: (0, 0, tb1(s)))
    tok_packed = pl.BlockSpec((PEER_HEADS, tn // LANE, PEER_N_KEYS // 2, LANE),
                              lambda s: (0, tb1(s), 0, 0))
    return pl.pallas_call(
        _experts_body,
        grid=(nb * EXP_TILES + EXP_LAG,),
        in_specs=[
            pl.BlockSpec((D_MODEL // 2, tn), lambda s: (0, tb0(s))),
            tok, tok, tok_packed, tok_packed,
            pl.BlockSpec((te // 2, D_MODEL), lambda s: (tile(0)(s), 0)),
            pl.BlockSpec((D_MODEL // 2, te), lambda s: (0, tile(2)(s))),
            pl.BlockSpec((tn, D_MODEL), lambda s: (tb2(s), 0)),
            pl.BlockSpec((1, D_MODEL), lambda s: (0, 0)),
        ],
        out_specs=pl.BlockSpec((tn, D_MODEL), lambda s: (tb2(s), 0)),
        out_shape=jax.ShapeDtypeStruct((n, D_MODEL), F32),
        scratch_shapes=[
            pltpu.VMEM((D_MODEL, tn), F32),
            pltpu.VMEM((te, tn), F32),
            pltpu.VMEM((te, tn), F32),
            pltpu.VMEM((te // 2, tn), U32),
            pltpu.VMEM((te // 2, tn), U32),
            pltpu.VMEM((PEER_HEADS, EXP_I, tn), F32),
            pltpu.VMEM((PEER_HEADS, EXP_I, tn), F32),
        ],
        compiler_params=pltpu.CompilerParams(
            dimension_semantics=("arbitrary",), vmem_limit_bytes=VMEM_LIMIT),
        name="experts",
    )(xt, lcnt, e1, r2, e2, u_packed, vt_packed, h1, fgain)


PACK_ROWS = 512


def _pack_rows_body(w_ref, o_ref):
    o_ref[...] = pltpu.bitcast(w_ref[...].astype(BF16), U32)


def _pack_rows(w):
    r, c = w.shape
    return pl.pallas_call(
        _pack_rows_body,
        grid=(r // PACK_ROWS,),
        in_specs=[pl.BlockSpec((PACK_ROWS, c), lambda i: (i, 0))],
        out_specs=pl.BlockSpec((PACK_ROWS // 2, c), lambda i: (i, 0)),
        out_shape=jax.ShapeDtypeStruct((r // 2, c), U32),
        compiler_params=pltpu.CompilerParams(
            dimension_semantics=("arbitrary",), vmem_limit_bytes=VMEM_LIMIT),
        name="pack_rows",
    )(w)


def _pack_transposed_body(w_ref, o_ref):
    o_ref[...] = pltpu.bitcast(w_ref[...].T.astype(BF16), U32)


def _pack_transposed(w):
    r, c = w.shape
    return pl.pallas_call(
        _pack_transposed_body,
        grid=(r // PACK_ROWS,),
        in_specs=[pl.BlockSpec((PACK_ROWS, c), lambda i: (i, 0))],
        out_specs=pl.BlockSpec((c // 2, PACK_ROWS), lambda i: (0, i)),
        out_shape=jax.ShapeDtypeStruct((c // 2, r), U32),
        compiler_params=pltpu.CompilerParams(
            dimension_semantics=("arbitrary",), vmem_limit_bytes=VMEM_LIMIT),
        name="pack_transposed",
    )(w)


def _block_diag(w):
    nh, d, _ = w.shape
    eye = jnp.eye(nh, dtype=w.dtype)
    return (eye[:, None, :, None] * w[:, :, None, :]).reshape(nh * d, nh * d)


def kernel(x, meta_tokens, norm1_gain, w_in, conv_w, conv_b, lru_gate_a_w, lru_gate_a_b, lru_gate_x_w, lru_gate_x_b, lru_lambda, lru_out_gain, rwkv_shift_mu, rwkv_w0, rwkv_w2, rwkv_a0, rwkv_a2, rwkv_g2, rwkv_k_k, rwkv_k_a, rwkv_r_k, rwkv_gn_w, rwkv_gn_b, w_out, norm2_gain, peer_w_query, peer_q_gain, peer_sub_keys, peer_u, peer_v, final_norm_gain):
    bsz, seq, _ = x.shape
    row = lambda v: v.reshape(1, -1).astype(F32)

    w_in_bf = w_in[0].astype(BF16)
    w_lru, w_rwkv = w_in_bf[:, :2 * D_LRU], w_in_bf[:, 2 * D_LRU:]
    wa_bd = _block_diag(lru_gate_a_w[0]).astype(BF16)
    wx_bd = _block_diag(lru_gate_x_w[0]).astype(BF16)
    zeros_l = jnp.zeros((D_DECAY_LORA, D_RWKV), F32)
    lora_bd = jnp.concatenate(
        [jnp.concatenate([rwkv_w2[0], zeros_l], axis=1),
         jnp.concatenate([zeros_l, rwkv_a2[0]], axis=1)], axis=0).astype(BF16)
    ones_bd = _block_diag(jnp.ones((RWKV_HEADS, RWKV_HEAD_DIM, RWKV_HEAD_DIM), F32)).astype(BF16)
    tri = jnp.tril(jnp.ones((CHUNK, CHUNK), F32)).astype(BF16)
    wo_bf = w_out[0].astype(BF16)
    wq_bf = peer_w_query[0].astype(BF16)
    u_packed = _pack_rows(peer_u[0])
    vt_packed = _pack_transposed(peer_v[0])

    head = jnp.concatenate(
        [jnp.zeros((FRONT_PAD, D_MODEL), x.dtype), meta_tokens.astype(x.dtype)], axis=0)
    h0 = jnp.concatenate([jnp.broadcast_to(head[None], (bsz, MIX_BLOCK, D_MODEL)), x], axis=1)
    t_pad = seq + MIX_BLOCK
    p_lru, p_rwkv = _inproj(h0.reshape(bsz * t_pad, D_MODEL), row(norm1_gain[0]), w_lru, w_rwkv)
    y_lru = _lru(p_lru.reshape(bsz, t_pad, 2 * D_LRU), conv_w[0], row(conv_b[0]), wa_bd,
                 row(lru_gate_a_b[0]), wx_bd, row(lru_gate_x_b[0]), row(lru_lambda[0]),
                 row(lru_out_gain[0]), seq)
    y_rwkv = _rwkv(p_rwkv.reshape(bsz, t_pad, D_RWKV_IN), row(rwkv_shift_mu[0]), row(rwkv_w0[0]),
                   lora_bd, row(rwkv_a0[0]), rwkv_g2[0].astype(BF16), row(rwkv_k_k[0]),
                   row(rwkv_k_a[0]), row(rwkv_r_k[0]), row(rwkv_gn_w[0]), row(rwkv_gn_b[0]),
                   ones_bd, tri, seq)

    n = bsz * seq
    h1, xt, lcnt, e1, r2, e2 = _post(
        y_lru.reshape(n, D_LRU), y_rwkv.reshape(n, D_RWKV), x.reshape(n, D_MODEL),
        wo_bf[:D_LRU], wo_bf[D_LRU:], row(norm2_gain[0]), wq_bf, row(peer_q_gain[0]),
        peer_sub_keys[0, :, 0], peer_sub_keys[0, :, 1])
    out = _experts(xt, lcnt, e1, r2, e2, u_packed, vt_packed, h1, row(final_norm_gain))
    return out.reshape(bsz, seq, D_MODEL)
```

```python
import functools

import jax
import jax.numpy as jnp
from jax import lax
from jax.experimental import pallas as pl
from jax.experimental.pallas import tpu as pltpu

F32 = jnp.float32
BF16 = jnp.bfloat16
U32 = jnp.uint32

D_MODEL = 1024
N_META = 16
NORM_EPS = 1e-6
D_LRU = 512
D_RWKV = 512
LRU_HEADS = 8
LRU_HEAD_DIM = D_LRU // LRU_HEADS
LRU_C = 8.0
RWKV_HEAD_DIM = 64
RWKV_HEADS = D_RWKV // RWKV_HEAD_DIM
D_DECAY_LORA = 64
D_AAA_LORA = 64
D_GATE_LORA = 128
RWKV_GN_EPS = 64e-5
D_RWKV_IN = 3 * D_RWKV + D_DECAY_LORA + D_AAA_LORA + D_GATE_LORA
PEER_HEADS = 8
PEER_N_KEYS = 128
PEER_D_QUERY = 256
PEER_HALF = PEER_D_QUERY // 2
PEER_TOPK = 16
N_EXPERTS = PEER_N_KEYS * PEER_N_KEYS

MIX_BLOCK = 128
FRONT_PAD = MIX_BLOCK - N_META
CHUNK = 64
INPROJ_ROWS = 256
POST_ROWS = 256
POST_HEAD_GROUP = 2
EXP_TOKENS = 512
EXP_I = 8
LANE = 128
NEG_INF = float("-inf")
VMEM_LIMIT = 52 * 1024 * 1024


def _dot(a, b):
    return jnp.dot(a.astype(BF16), b.astype(BF16), preferred_element_type=F32)


def _dot_nt(a, b):
    return lax.dot_general(a.astype(BF16), b.astype(BF16), (((1,), (1,)), ((), ())),
                           preferred_element_type=F32)


def _dot_tn(a, b):
    return lax.dot_general(a.astype(BF16), b.astype(BF16), (((0,), (0,)), ((), ())),
                           preferred_element_type=F32)


def _split2(x):
    hi = x.astype(BF16)
    lo = (x - hi.astype(F32)).astype(BF16)
    return hi, lo


def _split3(x):
    hi = x.astype(BF16)
    r1 = x - hi.astype(F32)
    mid = r1.astype(BF16)
    lo = (r1 - mid.astype(F32)).astype(BF16)
    return hi, mid, lo


def _dot3(a, b):
    ah, al = _split2(a)
    bh, bl = _split2(b)
    return (jnp.dot(ah, bh, preferred_element_type=F32)
            + jnp.dot(al, bh, preferred_element_type=F32)
            + jnp.dot(ah, bl, preferred_element_type=F32))


def _dot3_nt(a, b):
    ah, al = _split2(a)
    bh, bl = _split2(b)
    dn = (((1,), (1,)), ((), ()))
    return (lax.dot_general(ah, bh, dn, preferred_element_type=F32)
            + lax.dot_general(al, bh, dn, preferred_element_type=F32)
            + lax.dot_general(ah, bl, dn, preferred_element_type=F32))


def _group_sum(x, ones_bd):
    hi, lo = _split2(x)
    return (jnp.dot(hi, ones_bd, preferred_element_type=F32)
            + jnp.dot(lo, ones_bd, preferred_element_type=F32))


def _softplus(x):
    return jnp.maximum(x, 0.0) + jnp.log1p(jnp.exp(-jnp.abs(x)))


def _sigmoid(x):
    return 1.0 / (1.0 + jnp.exp(-x))


def _gelu_tanh(x):
    return 0.5 * x * (1.0 + jnp.tanh(0.7978845608028654 * (x + 0.044715 * (x * x * x))))


def _gelu_erf(x):
    return 0.5 * x * (1.0 + lax.erf(x * 0.7071067811865476))


def _inproj_body(h_ref, g_ref, wl_ref, wr_ref, ol_ref, or_ref):
    x = h_ref[...]
    ms = jnp.mean(x * x, axis=-1, keepdims=True)
    xn = ((x * lax.rsqrt(ms + NORM_EPS)) * g_ref[...]).astype(BF16)
    ol_ref[...] = jnp.dot(xn, wl_ref[...], preferred_element_type=F32)
    or_ref[...] = jnp.dot(xn, wr_ref[...], preferred_element_type=F32)


def _inproj(h, gain, w_lru, w_rwkv):
    n = h.shape[0]
    tm = INPROJ_ROWS
    const = lambda i: (0, 0)
    return pl.pallas_call(
        _inproj_body,
        grid=(n // tm,),
        in_specs=[
            pl.BlockSpec((tm, D_MODEL), lambda i: (i, 0)),
            pl.BlockSpec((1, D_MODEL), const),
            pl.BlockSpec((D_MODEL, 2 * D_LRU), const),
            pl.BlockSpec((D_MODEL, D_RWKV_IN), const),
        ],
        out_specs=[
            pl.BlockSpec((tm, 2 * D_LRU), lambda i: (i, 0)),
            pl.BlockSpec((tm, D_RWKV_IN), lambda i: (i, 0)),
        ],
        out_shape=[
            jax.ShapeDtypeStruct((n, 2 * D_LRU), F32),
            jax.ShapeDtypeStruct((n, D_RWKV_IN), F32),
        ],
        compiler_params=pltpu.CompilerParams(
            dimension_semantics=("arbitrary",), vmem_limit_bytes=VMEM_LIMIT),
        name="inproj",
    )(h, gain, w_lru, w_rwkv)


def _lru_body(p_ref, cw_ref, cb_ref, wa_ref, ba_ref, wx_ref, bx_ref, lam_ref, gain_ref,
              o_ref, xext_ref, h_ref):
    t = pl.program_id(1)
    tb = MIX_BLOCK

    @pl.when(t == 0)
    def _():
        xext_ref[0:8, :] = jnp.zeros((8, D_LRU), F32)
        h_ref[...] = jnp.zeros_like(h_ref)

    x = p_ref[0, :, 0:D_LRU]
    gate = p_ref[0, :, D_LRU:2 * D_LRU]
    xext_ref[8:8 + tb, :] = x
    cw = cw_ref[...]
    xc = (cb_ref[...] + cw[3:4, :] * x
          + cw[2:3, :] * xext_ref[7:7 + tb, :]
          + cw[1:2, :] * xext_ref[6:6 + tb, :]
          + cw[0:1, :] * xext_ref[5:5 + tb, :])
    xext_ref[0:8, :] = x[tb - 8:tb, :]

    xcb = xc.astype(BF16)
    r = _sigmoid(jnp.dot(xcb, wa_ref[...], preferred_element_type=F32) + ba_ref[...])
    i = _sigmoid(jnp.dot(xcb, wx_ref[...], preferred_element_type=F32) + bx_ref[...])
    log_a = (-LRU_C) * r * _softplus(-lam_ref[...])
    a = jnp.exp(log_a)
    th = jnp.tanh(log_a)
    one_minus_a2 = (-2.0 * th) / (1.0 - th)
    b = jnp.sqrt(one_minus_a2) * (i * xc)
    rows = lax.broadcasted_iota(jnp.int32, (tb, D_LRU), 0)
    b = jnp.where(rows + t * tb >= FRONT_PAD, b, 0.0)

    d = 1
    while d < tb:
        keep = rows >= d
        a_sh = jnp.where(keep, pltpu.roll(a, d, axis=0), 1.0)
        b_sh = jnp.where(keep, pltpu.roll(b, d, axis=0), 0.0)
        b = a * b_sh + b
        a = a * a_sh
        d *= 2
    h = a * h_ref[...] + b
    h_ref[...] = h[tb - 1:tb, :]

    y = h * _gelu_tanh(gate)
    ms = jnp.mean(y * y, axis=-1, keepdims=True)
    o_ref[0] = ((y * lax.rsqrt(ms + NORM_EPS)) * gain_ref[...]).astype(BF16)


def _lru(p_lru, conv_w, conv_b, wa_bd, ba, wx_bd, bx, lam, gain, seq):
    bsz, t_pad, _ = p_lru.shape
    nt = t_pad // MIX_BLOCK
    c2 = lambda b, t: (0, 0)
    vec = pl.BlockSpec((1, D_LRU), c2)
    mat = pl.BlockSpec((D_LRU, D_LRU), c2)
    return pl.pallas_call(
        _lru_body,
        grid=(bsz, nt),
        in_specs=[
            pl.BlockSpec((1, MIX_BLOCK, 2 * D_LRU), lambda b, t: (b, t, 0)),
            pl.BlockSpec((4, D_LRU), c2), vec, mat, vec, mat, vec, vec, vec,
        ],
        out_specs=pl.BlockSpec((1, MIX_BLOCK, D_LRU), lambda b, t: (b, jnp.maximum(t - 1, 0), 0)),
        out_shape=jax.ShapeDtypeStruct((bsz, seq, D_LRU), BF16),
        scratch_shapes=[pltpu.VMEM((MIX_BLOCK + 8, D_LRU), F32), pltpu.VMEM((1, D_LRU), F32)],
        compiler_params=pltpu.CompilerParams(
            dimension_semantics=("arbitrary", "arbitrary"), vmem_limit_bytes=VMEM_LIMIT),
        name="lru",
    )(p_lru, conv_w, conv_b, wa_bd, ba, wx_bd, bx, lam, gain)


def _rwkv_body(p_ref, mu_ref, w0_ref, lora_ref, a0_ref, g2_ref, kk_ref, ka_ref, rk_ref,
               gnw_ref, gnb_ref, ones_ref, tri_ref, o_ref, carry_ref, s_ref):
    t = pl.program_id(1)
    tb = MIX_BLOCK
    hd = RWKV_HEAD_DIM

    @pl.when(t == 0)
    def _():
        carry_ref[...] = jnp.zeros_like(carry_ref)
        s_ref[...] = jnp.zeros_like(s_ref)

    p = p_ref[0]
    rows = lax.broadcasted_iota(jnp.int32, (tb, D_RWKV_IN), 0)
    prev = jnp.where(rows == 0, carry_ref[...], pltpu.roll(p, 1, axis=0))
    carry_ref[...] = p[tb - 1:tb, :]
    ps = p + (prev - p) * mu_ref[...]

    r = ps[:, 0:D_RWKV]
    k = ps[:, D_RWKV:2 * D_RWKV]
    v = ps[:, 2 * D_RWKV:3 * D_RWKV]
    lo = ps[:, 3 * D_RWKV:3 * D_RWKV + 128]
    gl = ps[:, 3 * D_RWKV + 128:D_RWKV_IN]
    lane = lax.broadcasted_iota(jnp.int32, (tb, 128), 1)
    lo_act = jnp.where(lane < D_DECAY_LORA, jnp.tanh(lo), lo)
    la = _dot(lo_act, lora_ref[...])
    log_w = -_softplus(-(w0_ref[...] + la[:, 0:D_RWKV])) - 0.5
    ld = -jnp.exp(log_w)
    a = _sigmoid(a0_ref[...] + la[:, D_RWKV:2 * D_RWKV])
    g = _dot(_sigmoid(gl), g2_ref[...])

    ones_bd = ones_ref[...]
    kkr = k * kk_ref[...]
    kk = kkr / jnp.maximum(jnp.sqrt(_group_sum(kkr * kkr, ones_bd)), 1e-12)
    k2 = k * (1.0 + (a - 1.0) * ka_ref[...])
    bonus = _group_sum(r * k2 * rk_ref[...], ones_bd) * v
    kka = kk * a

    tri = tri_ref[...]
    ri = lax.broadcasted_iota(jnp.int32, (CHUNK, CHUNK), 0)
    ci = lax.broadcasted_iota(jnp.int32, (CHUNK, CHUNK), 1)
    strict = ri > ci
    incl = ri >= ci
    eye = ri == ci

    nchunk = tb // CHUNK
    at_i, rt_i, v_i, gram, bhat_t, khat_t, wtot_i = [], [], [], [], [], [], []
    for c in range(nchunk):
        sl = slice(c * CHUNK, (c + 1) * CHUNK)
        ld_c = ld[sl]
        hi, mid, lw = _split3(ld_c)
        cum = (jnp.dot(tri, hi, preferred_element_type=F32)
               + jnp.dot(tri, mid, preferred_element_type=F32)
               + jnp.dot(tri, lw, preferred_element_type=F32))
        ltot = cum[CHUNK - 1:CHUNK, :]
        e_inv = jnp.exp(-cum)
        e_end = jnp.exp(ltot - cum)
        at = -kk[sl] * jnp.exp(cum - ld_c)
        rt = r[sl] * jnp.exp(cum)
        bt = kka[sl] * e_inv
        kt = k2[sl] * e_inv
        bhat = kka[sl] * e_end
        khat = k2[sl] * e_end
        wtot = jnp.exp(ltot)
        for h in range(RWKV_HEADS):
            hs = slice(h * hd, (h + 1) * hd)
            at_i.append(at[:, hs])
            rt_i.append(rt[:, hs])
            v_i.append(v[sl, hs])
            wtot_i.append(wtot[:, hs])
            gram.append(_dot_nt(jnp.concatenate([at[:, hs], rt[:, hs]], axis=0),
                                jnp.concatenate([bt[:, hs], kt[:, hs]], axis=0)))
            bhat_t.append(bhat[:, hs].T)
            khat_t.append(khat[:, hs].T)

    rng = range(nchunk * RWKV_HEADS)
    pw = [jnp.where(strict, gram[i][0:CHUNK, 0:CHUNK], 0.0) for i in rng]
    a_ak = [jnp.where(strict, gram[i][0:CHUNK, CHUNK:2 * CHUNK], 0.0) for i in rng]
    a_rb = [jnp.where(incl, gram[i][CHUNK:2 * CHUNK, 0:CHUNK], 0.0) for i in rng]
    a_rk = [jnp.where(incl, gram[i][CHUNK:2 * CHUNK, CHUNK:2 * CHUNK], 0.0) for i in rng]
    akv = [_dot(a_ak[i], v_i[i]) for i in rng]
    uu = [jnp.concatenate([at_i[i], akv[i]], axis=1) for i in rng]
    for level in range(6):
        if level < 5:
            prod = [_dot(pw[i], jnp.concatenate([uu[i], pw[i]], axis=1)) for i in rng]
            pw = [prod[i][:, 2 * hd:3 * hd] for i in rng]
            uu = [uu[i] + prod[i][:, 0:2 * hd] for i in rng]
        else:
            uu = [uu[i] + _dot(pw[i], uu[i]) for i in rng]
    zeros_hd = jnp.zeros((CHUNK, hd), F32)
    tail = [_dot(jnp.concatenate([jnp.concatenate([a_rb[i], a_rk[i]], axis=1),
                                  jnp.concatenate([bhat_t[i], khat_t[i]], axis=1)], axis=0),
                 jnp.concatenate([uu[i], jnp.concatenate([zeros_hd, v_i[i]], axis=1)], axis=0))
            for i in rng]
    ra = [rt_i[i] + tail[i][0:CHUNK, 0:hd] for i in rng]
    ov = [tail[i][0:CHUNK, hd:2 * hd] for i in rng]
    m_i = [tail[i][CHUNK:2 * CHUNK, 0:hd] + jnp.where(eye, wtot_i[i], 0.0) for i in rng]
    sv = [tail[i][CHUNK:2 * CHUNK, hd:2 * hd] for i in rng]

    s_all = s_ref[...]
    s_h = [s_all[:, h * hd:(h + 1) * hd] for h in range(RWKV_HEADS)]
    for c in range(nchunk):
        sl = slice(c * CHUNK, (c + 1) * CHUNK)
        base = c * RWKV_HEADS
        outs = [_dot(ra[base + h], s_h[h]) + ov[base + h] for h in range(RWKV_HEADS)]
        s_h = [_dot3(m_i[base + h], s_h[h]) + sv[base + h] for h in range(RWKV_HEADS)]
        o = jnp.concatenate(outs, axis=1)
        mean = _group_sum(o, ones_bd) * (1.0 / hd)
        dev = o - mean
        var = _group_sum(dev * dev, ones_bd) * (1.0 / hd)
        on = dev * lax.rsqrt(var + RWKV_GN_EPS) * gnw_ref[...] + gnb_ref[...] + bonus[sl]
        o_ref[0, sl, :] = (on * g[sl]).astype(BF16)
    s_ref[...] = jnp.concatenate(s_h, axis=1)


def _rwkv(p_rwkv, mu, w0, lora_bd, a0, g2, k_k, k_a, r_k, gn_w, gn_b, ones_bd, tri, seq):
    bsz, t_pad, _ = p_rwkv.shape
    nt = t_pad // MIX_BLOCK
    c2 = lambda b, t: (0, 0)
    vec = pl.BlockSpec((1, D_RWKV), c2)
    return pl.pallas_call(
        _rwkv_body,
        grid=(bsz, nt),
        in_specs=[
            pl.BlockSpec((1, MIX_BLOCK, D_RWKV_IN), lambda b, t: (b, t, 0)),
            pl.BlockSpec((1, D_RWKV_IN), c2), vec,
            pl.BlockSpec((128, 2 * D_RWKV), c2), vec,
            pl.BlockSpec((D_GATE_LORA, D_RWKV), c2), vec, vec, vec, vec, vec,
            pl.BlockSpec((D_RWKV, D_RWKV), c2),
            pl.BlockSpec((CHUNK, CHUNK), c2),
        ],
        out_specs=pl.BlockSpec((1, MIX_BLOCK, D_RWKV), lambda b, t: (b, jnp.maximum(t - 1, 0), 0)),
        out_shape=jax.ShapeDtypeStruct((bsz, seq, D_RWKV), BF16),
        scratch_shapes=[pltpu.VMEM((1, D_RWKV_IN), F32), pltpu.VMEM((RWKV_HEAD_DIM, D_RWKV), F32)],
        compiler_params=pltpu.CompilerParams(
            dimension_semantics=("arbitrary", "arbitrary"), vmem_limit_bytes=VMEM_LIMIT),
        name="rwkv",
    )(p_rwkv, mu, w0, lora_bd, a0, g2, k_k, k_a, r_k, gn_w, gn_b, ones_bd, tri)


_CAND_COUNTS = tuple(PEER_TOPK // (a + 1) for a in range(PEER_TOPK))
_CAND_ROWS = -(-sum(_CAND_COUNTS) // 8) * 8


def _sort16_network():
    pairs = []

    def merge(lo, n, r):
        step = r * 2
        if step < n:
            merge(lo, n, step)
            merge(lo + r, n, step)
            pairs.extend((i, i + r) for i in range(lo + r, lo + n - r, step))
        else:
            pairs.append((lo, lo + r))

    def sort(lo, n):
        if n > 1:
            sort(lo, n // 2)
            sort(lo + n // 2, n // 2)
            merge(lo, n, 1)

    sort(0, PEER_TOPK)
    return tuple(pairs)


_SORT16 = _sort16_network()


def _top16_sorted(streams):
    all_slabs = []
    for s, _ in streams:
        slabs = [s[8 * k:8 * (k + 1), :] for k in range(PEER_N_KEYS // 8)]
        for i, j in _SORT16:
            slabs[i], slabs[j] = jnp.maximum(slabs[i], slabs[j]), jnp.minimum(slabs[i], slabs[j])
        all_slabs.append(slabs)
    for a in range(PEER_TOPK):
        for slabs, (_, v_ref) in zip(all_slabs, streams):
            m = jnp.max(slabs[0], axis=0, keepdims=True)
            v_ref[a:a + 1, :] = m
            hit = slabs[0] == m
            for k in range(PEER_TOPK - 1 - a):
                slabs[k] = jnp.where(hit, slabs[k + 1], slabs[k])


def _post_body(yl_ref, yr_ref, x_ref, woa_ref, wob_ref, g2_ref, wq_ref, qg_ref, k1_ref, k2_ref,
               h1_ref, xt_ref, l_ref, e1_ref, r2_ref, e2_ref, q_s, v1_s, v2_s, cand_s):
    tm = POST_ROWS
    h1 = (x_ref[...] + jnp.dot(yl_ref[...], woa_ref[...], preferred_element_type=F32)
          + jnp.dot(yr_ref[...], wob_ref[...], preferred_element_type=F32))
    h1_ref[...] = h1
    ms = jnp.mean(h1 * h1, axis=-1, keepdims=True)
    xn = (h1 * lax.rsqrt(ms + NORM_EPS)) * g2_ref[...]
    xt_ref[...] = pltpu.bitcast(xn.T.astype(BF16), U32)
    q_s[...] = jnp.dot(xn.astype(BF16), wq_ref[...], preferred_element_type=F32)
    cand_s[...] = jnp.full(cand_s.shape, NEG_INF, F32)

    n_lb = tm // LANE

    def head(hg, carry):
        pieces = range(POST_HEAD_GROUP * n_lb)
        head_of = [hg * POST_HEAD_GROUP + p // n_lb for p in pieces]
        s1, s2 = [], []
        for p in pieces:
            off = pl.multiple_of(head_of[p] * PEER_D_QUERY, PEER_D_QUERY)
            tl = p % n_lb
            qh = q_s[tl * LANE:(tl + 1) * LANE, pl.ds(off, PEER_D_QUERY)]
            msq = jnp.mean(qh * qh, axis=-1, keepdims=True)
            qn = (qh * lax.rsqrt(msq + NORM_EPS)) * qg_ref[:, pl.ds(off, PEER_D_QUERY)]
            s1.append(_dot3_nt(k1_ref[head_of[p]], qn[:, 0:PEER_HALF]))
            s2.append(_dot3_nt(k2_ref[head_of[p]], qn[:, PEER_HALF:PEER_D_QUERY]))
        _top16_sorted([(s1[p], v1_s.at[p]) for p in pieces] + [(s2[p], v2_s.at[p]) for p in pieces])
        for p in pieces:
            row = 0
            for a, nb in enumerate(_CAND_COUNTS):
                cand_s[p, row:row + nb, :] = v1_s[p, a:a + 1, :] + v2_s[p, 0:nb, :]
                row += nb
        cand = [cand_s[p] for p in pieces]
        m0 = [jnp.max(cand[p], axis=0, keepdims=True) for p in pieces]
        m = list(m0)
        z = [jnp.zeros((1, LANE), F32) for p in pieces]
        for k in range(PEER_TOPK):
            for p in pieces:
                if k:
                    m[p] = jnp.max(cand[p], axis=0, keepdims=True)
                z[p] = z[p] + jnp.exp(m[p] - m0[p])
                cand[p] = jnp.where(cand[p] == m[p], NEG_INF, cand[p])
        for p in pieces:
            h = head_of[p]
            tl = p % n_lb
            tok = slice(tl * LANE, (tl + 1) * LANE)
            tau = m[p]
            rank2 = jnp.full(s2[p].shape, float(PEER_TOPK), F32)
            for b in reversed(range(PEER_TOPK)):
                rank2 = jnp.where(s2[p] >= v2_s[p, b:b + 1, :], float(b), rank2)
            cnt = jnp.zeros(s1[p].shape, F32)
            for b in range(PEER_TOPK):
                cnt = jnp.where(s1[p] + v2_s[p, b:b + 1, :] >= tau, float(b + 1), cnt)
            l_ref[h, :, tok] = cnt
            e1_ref[h, :, tok] = jnp.exp(s1[p] - v1_s[p, 0:1, :]) * (1.0 / z[p])
            r2_ref[h, tl] = pltpu.bitcast(rank2.astype(BF16), U32)
            e2_ref[h, tl] = pltpu.bitcast(jnp.exp(s2[p] - v2_s[p, 0:1, :]).astype(BF16), U32)
        return carry

    lax.fori_loop(0, PEER_HEADS // POST_HEAD_GROUP, head, 0)


def _post(y_lru, y_rwkv, x, wo_a, wo_b, g2, wq, qg, k1, k2):
    n = x.shape[0]
    tm = POST_ROWS
    c2 = lambda i: (0, 0)
    c3 = lambda i: (0, 0, 0)
    hk = (PEER_HEADS, PEER_N_KEYS)
    tok = pl.BlockSpec(hk + (tm,), lambda i: (0, 0, i))
    tok_packed = pl.BlockSpec((PEER_HEADS, tm // LANE, PEER_N_KEYS // 2, LANE),
                              lambda i: (0, i, 0, 0))
    return pl.pallas_call(
        _post_body,
        grid=(n // tm,),
        in_specs=[
            pl.BlockSpec((tm, D_LRU), lambda i: (i, 0)),
            pl.BlockSpec((tm, D_RWKV), lambda i: (i, 0)),
            pl.BlockSpec((tm, D_MODEL), lambda i: (i, 0)),
            pl.BlockSpec((D_LRU, D_MODEL), c2),
            pl.BlockSpec((D_RWKV, D_MODEL), c2),
            pl.BlockSpec((1, D_MODEL), c2),
            pl.BlockSpec((D_MODEL, PEER_HEADS * PEER_D_QUERY), c2),
            pl.BlockSpec((1, PEER_HEADS * PEER_D_QUERY), c2),
            pl.BlockSpec(hk + (PEER_HALF,), c3),
            pl.BlockSpec(hk + (PEER_HALF,), c3),
        ],
        out_specs=[
            pl.BlockSpec((tm, D_MODEL), lambda i: (i, 0)),
            pl.BlockSpec((D_MODEL // 2, tm), lambda i: (0, i)),
            tok, tok, tok_packed, tok_packed,
        ],
        out_shape=[
            jax.ShapeDtypeStruct((n, D_MODEL), F32),
            jax.ShapeDtypeStruct((D_MODEL // 2, n), U32),
            jax.ShapeDtypeStruct(hk + (n,), F32),
            jax.ShapeDtypeStruct(hk + (n,), F32),
            jax.ShapeDtypeStruct((PEER_HEADS, n // LANE, PEER_N_KEYS // 2, LANE), U32),
            jax.ShapeDtypeStruct((PEER_HEADS, n // LANE, PEER_N_KEYS // 2, LANE), U32),
        ],
        scratch_shapes=[
            pltpu.VMEM((tm, PEER_HEADS * PEER_D_QUERY), F32),
            pltpu.VMEM((POST_HEAD_GROUP * tm // LANE, PEER_TOPK, LANE), F32),
            pltpu.VMEM((POST_HEAD_GROUP * tm // LANE, PEER_TOPK, LANE), F32),
            pltpu.VMEM((POST_HEAD_GROUP * tm // LANE, _CAND_ROWS, LANE), F32),
        ],
        compiler_params=pltpu.CompilerParams(
            dimension_semantics=("arbitrary",), vmem_limit_bytes=VMEM_LIMIT),
        name="post",
    )(y_lru, y_rwkv, x, wo_a, wo_b, g2, wq, qg, k1, k2)


def _row_bcast_bf16(rows8, ii):
    packed = jnp.broadcast_to(rows8[ii:ii + 1, :], (16, LANE)).astype(BF16)
    return jnp.tile(packed, (PEER_N_KEYS // 16, 1))


EXP_TILES = N_EXPERTS // (EXP_I * PEER_N_KEYS)
EXP_LAG = 2
EXP_MROWS = 128


def _experts_body(xt_ref, l_ref, e1_ref, r2_ref, e2_ref, u_ref, vt_ref, h1_ref, fg_ref, o_ref,
                  acc_ref, ht0_ref, ht1_ref, pt0_ref, pt1_ref, lrow_ref, erow_ref):
    s = pl.program_id(0)
    tn = EXP_TOKENS
    half = tn // 2

    @pl.when(s == 0)
    def _():
        for ref in (acc_ref, ht0_ref, ht1_ref, pt0_ref, pt1_ref):
            ref[...] = jnp.zeros_like(ref)

    e_gate = (s + EXP_TILES - 1) % EXP_TILES
    e_out = (s + EXP_TILES - 2) % EXP_TILES
    i0 = pl.multiple_of(e_gate * EXP_I, EXP_I)

    for h in range(PEER_HEADS):
        lrow_ref[h] = l_ref[h, pl.ds(i0, EXP_I), :]
        erow_ref[h] = e1_ref[h, pl.ds(i0, EXP_I), :]

    def stage(ht_w, ht_r, pt_w, pt_r):
        def hidden_job(mi, hf):
            rs = slice(mi * EXP_MROWS, (mi + 1) * EXP_MROWS)
            hl = slice(hf * half, (hf + 1) * half)
            rp = slice(mi * EXP_MROWS // 2, (mi + 1) * EXP_MROWS // 2)
            ht_w[rs, hl] = jnp.dot(pltpu.bitcast(u_ref[rp, :], BF16),
                                   pltpu.bitcast(xt_ref[:, hl], BF16), preferred_element_type=F32)

        def output_job(mi, hf):
            rs = slice(mi * EXP_MROWS, (mi + 1) * EXP_MROWS)
            hl = slice(hf * half, (hf + 1) * half)
            rp = slice(mi * EXP_MROWS // 2, (mi + 1) * EXP_MROWS // 2)
            acc_ref[rs, hl] += jnp.dot(pltpu.bitcast(vt_ref[rp, :], BF16),
                                       pltpu.bitcast(pt_r[:, hl], BF16), preferred_element_type=F32)

        def gate_piece(lb, ii):
            lanes = slice(lb * LANE, (lb + 1) * LANE)
            rows = slice(ii * PEER_N_KEYS, (ii + 1) * PEER_N_KEYS)
            rows_packed = slice(ii * PEER_N_KEYS // 2, (ii + 1) * PEER_N_KEYS // 2)
            gate = jnp.zeros((PEER_N_KEYS, LANE), BF16)
            for h in range(PEER_HEADS):
                lrow = _row_bcast_bf16(lrow_ref[h, :, lanes], ii)
                erow = _row_bcast_bf16(erow_ref[h, :, lanes], ii)
                sel = pltpu.bitcast(r2_ref[h, lb], BF16) < lrow
                val = pltpu.bitcast(e2_ref[h, lb], BF16) * erow
                gate = gate + jnp.where(sel, val, jnp.zeros_like(val))
            pt = gate * _gelu_erf(ht_r[rows, lanes]).astype(BF16)
            pt_w[rows_packed, lanes] = pltpu.bitcast(pt, U32)

        nm = (EXP_I * PEER_N_KEYS) // EXP_MROWS
        mxu_jobs = ([functools.partial(hidden_job, mi, hf) for hf in range(2) for mi in range(nm)]
                    + [functools.partial(output_job, mi, hf) for hf in range(2)
                       for mi in range(D_MODEL // EXP_MROWS)])
        pieces = [functools.partial(gate_piece, lb, ii)
                  for lb in range(tn // LANE) for ii in range(EXP_I)]
        per_job = -(-len(pieces) // len(mxu_jobs))
        for j, job in enumerate(mxu_jobs):
            job()
            for piece in pieces[j * per_job:(j + 1) * per_job]:
                piece()

    @pl.when(s % 2 == 0)
    def _():
        stage(ht0_ref, ht1_ref, pt0_ref, pt1_ref)

    @pl.when(s % 2 == 1)
    def _():
        stage(ht1_ref, ht0_ref, pt1_ref, pt0_ref)

    @pl.when(jnp.logical_and(e_out == EXP_TILES - 1, s >= EXP_LAG))
    def _():
        h2 = h1_ref[...] + acc_ref[...].T
        ms = jnp.mean(h2 * h2, axis=-1, keepdims=True)
        o_ref[...] = (h2 * lax.rsqrt(ms + NORM_EPS)) * fg_ref[...]
        acc_ref[...] = jnp.zeros_like(acc_ref)


def _experts(xt, lcnt, e1, r2, e2, u_packed, vt_packed, h1, fgain):
    n = h1.shape[0]
    tn = EXP_TOKENS
    te = EXP_I * PEER_N_KEYS
    nb = n // tn
    hk = (PEER_HEADS, PEER_N_KEYS)

    def tok_block(lag):
        return lambda s: jnp.clip((s - lag) // EXP_TILES, 0, nb - 1)

    def tile(lag):
        return lambda s: (s + EXP_TILES - lag) % EXP_TILES

    tb0, tb1, tb2 = tok_block(0), tok_block(1), tok_block(2)
    tok = pl.BlockSpec(hk + (tn,), lambda s: (0, 0, tb1(s)))
    tok_packed = pl.BlockSpec((PEER_HEADS, tn // LANE, PEER_N_KEYS // 2, LANE),
                              lambda s: (0, tb1(s), 0, 0))
    return pl.pallas_call(
        _experts_body,
        grid=(nb * EXP_TILES + EXP_LAG,),
        in_specs=[
            pl.BlockSpec((D_MODEL // 2, tn), lambda s: (0, tb0(s))),
            tok, tok, tok_packed, tok_packed,
            pl.BlockSpec((te // 2, D_MODEL), lambda s: (tile(0)(s), 0)),
            pl.BlockSpec((D_MODEL // 2, te), lambda s: (0, tile(2)(s))),
            pl.BlockSpec((tn, D_MODEL), lambda s: (tb2(s), 0)),
            pl.BlockSpec((1, D_MODEL), lambda s: (0, 0)),
        ],
        out_specs=pl.BlockSpec((tn, D_MODEL), lambda s: (tb2(s), 0)),
        out_shape=jax.ShapeDtypeStruct((n, D_MODEL), F32),
        scratch_shapes=[
            pltpu.VMEM((D_MODEL, tn), F32),
            pltpu.VMEM((te, tn), F32),
            pltpu.VMEM((te, tn), F32),
            pltpu.VMEM((te // 2, tn), U32),
            pltpu.VMEM((te // 2, tn), U32),
            pltpu.VMEM((PEER_HEADS, EXP_I, tn), F32),
            pltpu.VMEM((PEER_HEADS, EXP_I, tn), F32),
        ],
        compiler_params=pltpu.CompilerParams(
            dimension_semantics=("arbitrary",), vmem_limit_bytes=VMEM_LIMIT),
        name="experts",
    )(xt, lcnt, e1, r2, e2, u_packed, vt_packed, h1, fgain)


PACK_ROWS = 512


def _pack_rows_body(w_ref, o_ref):
    o_ref[...] = pltpu.bitcast(w_ref[...].astype(BF16), U32)


def _pack_rows(w):
    r, c = w.shape
    return pl.pallas_call(
        _pack_rows_body,
        grid=(r // PACK_ROWS,),
        in_specs=[pl.BlockSpec((PACK_ROWS, c), lambda i: (i, 0))],
        out_specs=pl.BlockSpec((PACK_ROWS // 2, c), lambda i: (i, 0)),
        out_shape=jax.ShapeDtypeStruct((r // 2, c), U32),
        compiler_params=pltpu.CompilerParams(
            dimension_semantics=("arbitrary",), vmem_limit_bytes=VMEM_LIMIT),
        name="pack_rows",
    )(w)


def _pack_transposed_body(w_ref, o_ref):
    o_ref[...] = pltpu.bitcast(w_ref[...].T.astype(BF16), U32)


def _pack_transposed(w):
    r, c = w.shape
    return pl.pallas_call(
        _pack_transposed_body,
        grid=(r // PACK_ROWS,),
        in_specs=[pl.BlockSpec((PACK_ROWS, c), lambda i: (i, 0))],
        out_specs=pl.BlockSpec((c // 2, PACK_ROWS), lambda i: (0, i)),
        out_shape=jax.ShapeDtypeStruct((c // 2, r), U32),
        compiler_params=pltpu.CompilerParams(
            dimension_semantics=("arbitrary",), vmem_limit_bytes=VMEM_LIMIT),
        name="pack_transposed",
    )(w)


def _block_diag(w):
    nh, d, _ = w.shape
    eye = jnp.eye(nh, dtype=w.dtype)
    return (eye[:, None, :, None] * w[:, :, None, :]).reshape(nh * d, nh * d)


def kernel(x, meta_tokens, norm1_gain, w_in, conv_w, conv_b, lru_gate_a_w, lru_gate_a_b, lru_gate_x_w, lru_gate_x_b, lru_lambda, lru_out_gain, rwkv_shift_mu, rwkv_w0, rwkv_w2, rwkv_a0, rwkv_a2, rwkv_g2, rwkv_k_k, rwkv_k_a, rwkv_r_k, rwkv_gn_w, rwkv_gn_b, w_out, norm2_gain, peer_w_query, peer_q_gain, peer_sub_keys, peer_u, peer_v, final_norm_gain):
    bsz, seq, _ = x.shape
    row = lambda v: v.reshape(1, -1).astype(F32)

    w_in_bf = w_in[0].astype(BF16)
    w_lru, w_rwkv = w_in_bf[:, :2 * D_LRU], w_in_bf[:, 2 * D_LRU:]
    wa_bd = _block_diag(lru_gate_a_w[0]).astype(BF16)
    wx_bd = _block_diag(lru_gate_x_w[0]).astype(BF16)
    zeros_l = jnp.zeros((D_DECAY_LORA, D_RWKV), F32)
    lora_bd = jnp.concatenate(
        [jnp.concatenate([rwkv_w2[0], zeros_l], axis=1),
         jnp.concatenate([zeros_l, rwkv_a2[0]], axis=1)], axis=0).astype(BF16)
    ones_bd = _block_diag(jnp.ones((RWKV_HEADS, RWKV_HEAD_DIM, RWKV_HEAD_DIM), F32)).astype(BF16)
    tri = jnp.tril(jnp.ones((CHUNK, CHUNK), F32)).astype(BF16)
    wo_bf = w_out[0].astype(BF16)
    wq_bf = peer_w_query[0].astype(BF16)
    u_packed = _pack_rows(peer_u[0])
    vt_packed = _pack_transposed(peer_v[0])

    head = jnp.concatenate(
        [jnp.zeros((FRONT_PAD, D_MODEL), x.dtype), meta_tokens.astype(x.dtype)], axis=0)
    h0 = jnp.concatenate([jnp.broadcast_to(head[None], (bsz, MIX_BLOCK, D_MODEL)), x], axis=1)
    t_pad = seq + MIX_BLOCK
    p_lru, p_rwkv = _inproj(h0.reshape(bsz * t_pad, D_MODEL), row(norm1_gain[0]), w_lru, w_rwkv)
    y_lru = _lru(p_lru.reshape(bsz, t_pad, 2 * D_LRU), conv_w[0], row(conv_b[0]), wa_bd,
                 row(lru_gate_a_b[0]), wx_bd, row(lru_gate_x_b[0]), row(lru_lambda[0]),
                 row(lru_out_gain[0]), seq)
    y_rwkv = _rwkv(p_rwkv.reshape(bsz, t_pad, D_RWKV_IN), row(rwkv_shift_mu[0]), row(rwkv_w0[0]),
                   lora_bd, row(rwkv_a0[0]), rwkv_g2[0].astype(BF16), row(rwkv_k_k[0]),
                   row(rwkv_k_a[0]), row(rwkv_r_k[0]), row(rwkv_gn_w[0]), row(rwkv_gn_b[0]),
                   ones_bd, tri, seq)

    n = bsz * seq
    h1, xt, lcnt, e1, r2, e2 = _post(
        y_lru.reshape(n, D_LRU), y_rwkv.reshape(n, D_RWKV), x.reshape(n, D_MODEL),
        wo_bf[:D_LRU], wo_bf[D_LRU:], row(norm2_gain[0]), wq_bf, row(peer_q_gain[0]),
        peer_sub_keys[0, :, 0], peer_sub_keys[0, :, 1])
    out = _experts(xt, lcnt, e1, r2, e2, u_packed, vt_packed, h1, row(final_norm_gain))
    return out.reshape(bsz, seq, D_MODEL)
```

```python
import functools

import jax
import jax.numpy as jnp
from jax import lax
from jax.experimental import pallas as pl
from jax.experimental.pallas import tpu as pltpu

F32 = jnp.float32
BF16 = jnp.bfloat16
U32 = jnp.uint32

D_MODEL = 1024
N_META = 16
NORM_EPS = 1e-6
D_LRU = 512
D_RWKV = 512
LRU_HEADS = 8
LRU_HEAD_DIM = D_LRU // LRU_HEADS
LRU_C = 8.0
RWKV_HEAD_DIM = 64
RWKV_HEADS = D_RWKV // RWKV_HEAD_DIM
D_DECAY_LORA = 64
D_AAA_LORA = 64
D_GATE_LORA = 128
RWKV_GN_EPS = 64e-5
D_RWKV_IN = 3 * D_RWKV + D_DECAY_LORA + D_AAA_LORA + D_GATE_LORA
PEER_HEADS = 8
PEER_N_KEYS = 128
PEER_D_QUERY = 256
PEER_HALF = PEER_D_QUERY // 2
PEER_TOPK = 16
N_EXPERTS = PEER_N_KEYS * PEER_N_KEYS

MIX_BLOCK = 256
FRONT_PAD = MIX_BLOCK - N_META
CHUNK = 64
INPROJ_ROWS = 512
POST_ROWS = 512
POST_HEAD_GROUP = 2
EXP_TOKENS = 512
EXP_I = 8
LANE = 128
NEG_INF = float("-inf")
VMEM_LIMIT = 52 * 1024 * 1024


def _dot(a, b):
    return jnp.dot(a.astype(BF16), b.astype(BF16), preferred_element_type=F32)


def _dot_nt(a, b):
    return lax.dot_general(a.astype(BF16), b.astype(BF16), (((1,), (1,)), ((), ())),
                           preferred_element_type=F32)


def _dot_tn(a, b):
    return lax.dot_general(a.astype(BF16), b.astype(BF16), (((0,), (0,)), ((), ())),
                           preferred_element_type=F32)


def _split2(x):
    hi = x.astype(BF16)
    lo = (x - hi.astype(F32)).astype(BF16)
    return hi, lo


def _split3(x):
    hi = x.astype(BF16)
    r1 = x - hi.astype(F32)
    mid = r1.astype(BF16)
    lo = (r1 - mid.astype(F32)).astype(BF16)
    return hi, mid, lo


def _dot3(a, b):
    ah, al = _split2(a)
    bh, bl = _split2(b)
    return (jnp.dot(ah, bh, preferred_element_type=F32)
            + jnp.dot(al, bh, preferred_element_type=F32)
            + jnp.dot(ah, bl, preferred_element_type=F32))


def _dot3_nt(a, b):
    ah, al = _split2(a)
    bh, bl = _split2(b)
    dn = (((1,), (1,)), ((), ()))
    return (lax.dot_general(ah, bh, dn, preferred_element_type=F32)
            + lax.dot_general(al, bh, dn, preferred_element_type=F32)
            + lax.dot_general(ah, bl, dn, preferred_element_type=F32))


def _group_sum(x, ones_bd):
    hi, lo = _split2(x)
    return (jnp.dot(hi, ones_bd, preferred_element_type=F32)
            + jnp.dot(lo, ones_bd, preferred_element_type=F32))


def _softplus(x):
    return jnp.maximum(x, 0.0) + jnp.log1p(jnp.exp(-jnp.abs(x)))


def _sigmoid(x):
    return 1.0 / (1.0 + jnp.exp(-x))


def _gelu_tanh(x):
    return 0.5 * x * (1.0 + jnp.tanh(0.7978845608028654 * (x + 0.044715 * (x * x * x))))


def _gelu_erf(x):
    return 0.5 * x * (1.0 + lax.erf(x * 0.7071067811865476))


def _inproj_body(h_ref, g_ref, wl_ref, wr_ref, ol_ref, or_ref):
    x = h_ref[...]
    ms = jnp.mean(x * x, axis=-1, keepdims=True)
    xn = ((x * lax.rsqrt(ms + NORM_EPS)) * g_ref[...]).astype(BF16)
    ol_ref[...] = jnp.dot(xn, wl_ref[...], preferred_element_type=F32)
    or_ref[...] = jnp.dot(xn, wr_ref[...], preferred_element_type=F32)


def _inproj(h, gain, w_lru, w_rwkv):
    n = h.shape[0]
    tm = INPROJ_ROWS
    const = lambda i: (0, 0)
    return pl.pallas_call(
        _inproj_body,
        grid=(n // tm,),
        in_specs=[
            pl.BlockSpec((tm, D_MODEL), lambda i: (i, 0)),
            pl.BlockSpec((1, D_MODEL), const),
            pl.BlockSpec((D_MODEL, 2 * D_LRU), const),
            pl.BlockSpec((D_MODEL, D_RWKV_IN), const),
        ],
        out_specs=[
            pl.BlockSpec((tm, 2 * D_LRU), lambda i: (i, 0)),
            pl.BlockSpec((tm, D_RWKV_IN), lambda i: (i, 0)),
        ],
        out_shape=[
            jax.ShapeDtypeStruct((n, 2 * D_LRU), F32),
            jax.ShapeDtypeStruct((n, D_RWKV_IN), F32),
        ],
        compiler_params=pltpu.CompilerParams(
            dimension_semantics=("arbitrary",), vmem_limit_bytes=VMEM_LIMIT),
        name="inproj",
    )(h, gain, w_lru, w_rwkv)


def _lru_body(p_ref, cw_ref, cb_ref, wa_ref, ba_ref, wx_ref, bx_ref, lam_ref, gain_ref,
              o_ref, xext_ref, h_ref):
    t = pl.program_id(1)
    tb = MIX_BLOCK

    @pl.when(t == 0)
    def _():
        xext_ref[0:8, :] = jnp.zeros((8, D_LRU), F32)
        h_ref[...] = jnp.zeros_like(h_ref)

    x = p_ref[0, :, 0:D_LRU]
    gate = p_ref[0, :, D_LRU:2 * D_LRU]
    xext_ref[8:8 + tb, :] = x
    cw = cw_ref[...]
    xc = (cb_ref[...] + cw[3:4, :] * x
          + cw[2:3, :] * xext_ref[7:7 + tb, :]
          + cw[1:2, :] * xext_ref[6:6 + tb, :]
          + cw[0:1, :] * xext_ref[5:5 + tb, :])
    xext_ref[0:8, :] = x[tb - 8:tb, :]

    xcb = xc.astype(BF16)
    r = _sigmoid(jnp.dot(xcb, wa_ref[...], preferred_element_type=F32) + ba_ref[...])
    i = _sigmoid(jnp.dot(xcb, wx_ref[...], preferred_element_type=F32) + bx_ref[...])
    log_a = (-LRU_C) * r * _softplus(-lam_ref[...])
    a = jnp.exp(log_a)
    th = jnp.tanh(log_a)
    one_minus_a2 = (-2.0 * th) / (1.0 - th)
    b = jnp.sqrt(one_minus_a2) * (i * xc)
    rows = lax.broadcasted_iota(jnp.int32, (tb, D_LRU), 0)
    b = jnp.where(rows + t * tb >= FRONT_PAD, b, 0.0)

    d = 1
    while d < tb:
        keep = rows >= d
        a_sh = jnp.where(keep, pltpu.roll(a, d, axis=0), 1.0)
        b_sh = jnp.where(keep, pltpu.roll(b, d, axis=0), 0.0)
        b = a * b_sh + b
        a = a * a_sh
        d *= 2
    h = a * h_ref[...] + b
    h_ref[...] = h[tb - 1:tb, :]

    y = h * _gelu_tanh(gate)
    ms = jnp.mean(y * y, axis=-1, keepdims=True)
    o_ref[0] = ((y * lax.rsqrt(ms + NORM_EPS)) * gain_ref[...]).astype(BF16)


def _lru(p_lru, conv_w, conv_b, wa_bd, ba, wx_bd, bx, lam, gain, seq):
    bsz, t_pad, _ = p_lru.shape
    nt = t_pad // MIX_BLOCK
    c2 = lambda b, t: (0, 0)
    vec = pl.BlockSpec((1, D_LRU), c2)
    mat = pl.BlockSpec((D_LRU, D_LRU), c2)
    return pl.pallas_call(
        _lru_body,
        grid=(bsz, nt),
        in_specs=[
            pl.BlockSpec((1, MIX_BLOCK, 2 * D_LRU), lambda b, t: (b, t, 0)),
            pl.BlockSpec((4, D_LRU), c2), vec, mat, vec, mat, vec, vec, vec,
        ],
        out_specs=pl.BlockSpec((1, MIX_BLOCK, D_LRU), lambda b, t: (b, jnp.maximum(t - 1, 0), 0)),
        out_shape=jax.ShapeDtypeStruct((bsz, seq, D_LRU), BF16),
        scratch_shapes=[pltpu.VMEM((MIX_BLOCK + 8, D_LRU), F32), pltpu.VMEM((1, D_LRU), F32)],
        compiler_params=pltpu.CompilerParams(
            dimension_semantics=("arbitrary", "arbitrary"), vmem_limit_bytes=VMEM_LIMIT),
        name="lru",
    )(p_lru, conv_w, conv_b, wa_bd, ba, wx_bd, bx, lam, gain)


def _rwkv_body(p_ref, mu_ref, w0_ref, lora_ref, a0_ref, g2_ref, kk_ref, ka_ref, rk_ref,
               gnw_ref, gnb_ref, ones_ref, tri_ref, o_ref, carry_ref, s_ref):
    t = pl.program_id(1)
    tb = MIX_BLOCK
    hd = RWKV_HEAD_DIM

    @pl.when(t == 0)
    def _():
        carry_ref[...] = jnp.zeros_like(carry_ref)
        s_ref[...] = jnp.zeros_like(s_ref)

    p = p_ref[0]
    rows = lax.broadcasted_iota(jnp.int32, (tb, D_RWKV_IN), 0)
    prev = jnp.where(rows == 0, carry_ref[...], pltpu.roll(p, 1, axis=0))
    carry_ref[...] = p[tb - 1:tb, :]
    ps = p + (prev - p) * mu_ref[...]

    r = ps[:, 0:D_RWKV]
    k = ps[:, D_RWKV:2 * D_RWKV]
    v = ps[:, 2 * D_RWKV:3 * D_RWKV]
    lo = ps[:, 3 * D_RWKV:3 * D_RWKV + 128]
    gl = ps[:, 3 * D_RWKV + 128:D_RWKV_IN]
    lane = lax.broadcasted_iota(jnp.int32, (tb, 128), 1)
    lo_act = jnp.where(lane < D_DECAY_LORA, jnp.tanh(lo), lo)
    la = _dot(lo_act, lora_ref[...])
    log_w = -_softplus(-(w0_ref[...] + la[:, 0:D_RWKV])) - 0.5
    ld = -jnp.exp(log_w)
    a = _sigmoid(a0_ref[...] + la[:, D_RWKV:2 * D_RWKV])
    g = _dot(_sigmoid(gl), g2_ref[...])

    ones_bd = ones_ref[...]
    kkr = k * kk_ref[...]
    kk = kkr / jnp.maximum(jnp.sqrt(_group_sum(kkr * kkr, ones_bd)), 1e-12)
    k2 = k * (1.0 + (a - 1.0) * ka_ref[...])
    bonus = _group_sum(r * k2 * rk_ref[...], ones_bd) * v
    kka = kk * a

    tri = tri_ref[...]
    ri = lax.broadcasted_iota(jnp.int32, (CHUNK, CHUNK), 0)
    ci = lax.broadcasted_iota(jnp.int32, (CHUNK, CHUNK), 1)
    strict = ri > ci
    incl = ri >= ci
    eye = ri == ci

    nchunk = tb // CHUNK
    at_i, rt_i, v_i, gram, bhat_t, khat_t, wtot_i = [], [], [], [], [], [], []
    for c in range(nchunk):
        sl = slice(c * CHUNK, (c + 1) * CHUNK)
        ld_c = ld[sl]
        hi, mid, lw = _split3(ld_c)
        cum = (jnp.dot(tri, hi, preferred_element_type=F32)
               + jnp.dot(tri, mid, preferred_element_type=F32)
               + jnp.dot(tri, lw, preferred_element_type=F32))
        ltot = cum[CHUNK - 1:CHUNK, :]
        e_inv = jnp.exp(-cum)
        e_end = jnp.exp(ltot - cum)
        at = -kk[sl] * jnp.exp(cum - ld_c)
        rt = r[sl] * jnp.exp(cum)
        bt = kka[sl] * e_inv
        kt = k2[sl] * e_inv
        bhat = kka[sl] * e_end
        khat = k2[sl] * e_end
        wtot = jnp.exp(ltot)
        for h in range(RWKV_HEADS):
            hs = slice(h * hd, (h + 1) * hd)
            at_i.append(at[:, hs])
            rt_i.append(rt[:, hs])
            v_i.append(v[sl, hs])
            wtot_i.append(wtot[:, hs])
            gram.append(_dot_nt(jnp.concatenate([at[:, hs], rt[:, hs]], axis=0),
                                jnp.concatenate([bt[:, hs], kt[:, hs]], axis=0)))
            bhat_t.append(bhat[:, hs].T)
            khat_t.append(khat[:, hs].T)

    rng = range(nchunk * RWKV_HEADS)
    pw = [jnp.where(strict, gram[i][0:CHUNK, 0:CHUNK], 0.0) for i in rng]
    a_ak = [jnp.where(strict, gram[i][0:CHUNK, CHUNK:2 * CHUNK], 0.0) for i in rng]
    a_rb = [jnp.where(incl, gram[i][CHUNK:2 * CHUNK, 0:CHUNK], 0.0) for i in rng]
    a_rk = [jnp.where(incl, gram[i][CHUNK:2 * CHUNK, CHUNK:2 * CHUNK], 0.0) for i in rng]
    akv = [_dot(a_ak[i], v_i[i]) for i in rng]
    uu = [jnp.concatenate([at_i[i], akv[i]], axis=1) for i in rng]
    for level in range(6):
        if level < 5:
            prod = [_dot(pw[i], jnp.concatenate([uu[i], pw[i]], axis=1)) for i in rng]
            pw = [prod[i][:, 2 * hd:3 * hd] for i in rng]
            uu = [uu[i] + prod[i][:, 0:2 * hd] for i in rng]
        else:
            uu = [uu[i] + _dot(pw[i], uu[i]) for i in rng]
    zeros_hd = jnp.zeros((CHUNK, hd), F32)
    tail = [_dot(jnp.concatenate([jnp.concatenate([a_rb[i], a_rk[i]], axis=1),
                                  jnp.concatenate([bhat_t[i], khat_t[i]], axis=1)], axis=0),
                 jnp.concatenate([uu[i], jnp.concatenate([zeros_hd, v_i[i]], axis=1)], axis=0))
            for i in rng]
    ra = [rt_i[i] + tail[i][0:CHUNK, 0:hd] for i in rng]
    ov = [tail[i][0:CHUNK, hd:2 * hd] for i in rng]
    m_i = [tail[i][CHUNK:2 * CHUNK, 0:hd] + jnp.where(eye, wtot_i[i], 0.0) for i in rng]
    sv = [tail[i][CHUNK:2 * CHUNK, hd:2 * hd] for i in rng]

    s_all = s_ref[...]
    s_h = [s_all[:, h * hd:(h + 1) * hd] for h in range(RWKV_HEADS)]
    for c in range(nchunk):
        sl = slice(c * CHUNK, (c + 1) * CHUNK)
        base = c * RWKV_HEADS
        outs = [_dot(ra[base + h], s_h[h]) + ov[base + h] for h in range(RWKV_HEADS)]
        s_h = [_dot3(m_i[base + h], s_h[h]) + sv[base + h] for h in range(RWKV_HEADS)]
        o = jnp.concatenate(outs, axis=1)
        mean = _group_sum(o, ones_bd) * (1.0 / hd)
        dev = o - mean
        var = _group_sum(dev * dev, ones_bd) * (1.0 / hd)
        on = dev * lax.rsqrt(var + RWKV_GN_EPS) * gnw_ref[...] + gnb_ref[...] + bonus[sl]
        o_ref[0, sl, :] = (on * g[sl]).astype(BF16)
    s_ref[...] = jnp.concatenate(s_h, axis=1)


def _rwkv(p_rwkv, mu, w0, lora_bd, a0, g2, k_k, k_a, r_k, gn_w, gn_b, ones_bd, tri, seq):
    bsz, t_pad, _ = p_rwkv.shape
    nt = t_pad // MIX_BLOCK
    c2 = lambda b, t: (0, 0)
    vec = pl.BlockSpec((1, D_RWKV), c2)
    return pl.pallas_call(
        _rwkv_body,
        grid=(bsz, nt),
        in_specs=[
            pl.BlockSpec((1, MIX_BLOCK, D_RWKV_IN), lambda b, t: (b, t, 0)),
            pl.BlockSpec((1, D_RWKV_IN), c2), vec,
            pl.BlockSpec((128, 2 * D_RWKV), c2), vec,
            pl.BlockSpec((D_GATE_LORA, D_RWKV), c2), vec, vec, vec, vec, vec,
            pl.BlockSpec((D_RWKV, D_RWKV), c2),
            pl.BlockSpec((CHUNK, CHUNK), c2),
        ],
        out_specs=pl.BlockSpec((1, MIX_BLOCK, D_RWKV), lambda b, t: (b, jnp.maximum(t - 1, 0), 0)),
        out_shape=jax.ShapeDtypeStruct((bsz, seq, D_RWKV), BF16),
        scratch_shapes=[pltpu.VMEM((1, D_RWKV_IN), F32), pltpu.VMEM((RWKV_HEAD_DIM, D_RWKV), F32)],
        compiler_params=pltpu.CompilerParams(
            dimension_semantics=("arbitrary", "arbitrary"), vmem_limit_bytes=VMEM_LIMIT),
        name="rwkv",
    )(p_rwkv, mu, w0, lora_bd, a0, g2, k_k, k_a, r_k, gn_w, gn_b, ones_bd, tri)


_CAND_COUNTS = tuple(PEER_TOPK // (a + 1) for a in range(PEER_TOPK))
_CAND_ROWS = -(-sum(_CAND_COUNTS) // 8) * 8


def _sort16_network():
    pairs = []

    def merge(lo, n, r):
        step = r * 2
        if step < n:
            merge(lo, n, step)
            merge(lo + r, n, step)
            pairs.extend((i, i + r) for i in range(lo + r, lo + n - r, step))
        else:
            pairs.append((lo, lo + r))

    def sort(lo, n):
        if n > 1:
            sort(lo, n // 2)
            sort(lo + n // 2, n // 2)
            merge(lo, n, 1)

    sort(0, PEER_TOPK)
    return tuple(pairs)


_SORT16 = _sort16_network()


def _top16_sorted(streams):
    all_slabs = []
    for s, _ in streams:
        slabs = [s[8 * k:8 * (k + 1), :] for k in range(PEER_N_KEYS // 8)]
        for i, j in _SORT16:
            slabs[i], slabs[j] = jnp.maximum(slabs[i], slabs[j]), jnp.minimum(slabs[i], slabs[j])
        all_slabs.append(slabs)
    for a in range(PEER_TOPK):
        for slabs, (_, v_ref) in zip(all_slabs, streams):
            m = jnp.max(slabs[0], axis=0, keepdims=True)
            v_ref[a:a + 1, :] = m
            hit = slabs[0] == m
            for k in range(PEER_TOPK - 1 - a):
                slabs[k] = jnp.where(hit, slabs[k + 1], slabs[k])


def _post_body(yl_ref, yr_ref, x_ref, woa_ref, wob_ref, g2_ref, wq_ref, qg_ref, k1_ref, k2_ref,
               h1_ref, xt_ref, l_ref, e1_ref, r2_ref, e2_ref, q_s, v1_s, v2_s, cand_s):
    tm = POST_ROWS
    h1 = (x_ref[...] + jnp.dot(yl_ref[...], woa_ref[...], preferred_element_type=F32)
          + jnp.dot(yr_ref[...], wob_ref[...], preferred_element_type=F32))
    h1_ref[...] = h1
    ms = jnp.mean(h1 * h1, axis=-1, keepdims=True)
    xn = (h1 * lax.rsqrt(ms + NORM_EPS)) * g2_ref[...]
    xt_ref[...] = pltpu.bitcast(xn.T.astype(BF16), U32)
    q_s[...] = jnp.dot(xn.astype(BF16), wq_ref[...], preferred_element_type=F32)
    cand_s[...] = jnp.full(cand_s.shape, NEG_INF, F32)

    n_lb = tm // LANE

    def head(hg, carry):
        pieces = range(POST_HEAD_GROUP * n_lb)
        head_of = [hg * POST_HEAD_GROUP + p // n_lb for p in pieces]
        s1, s2 = [], []
        for p in pieces:
            off = pl.multiple_of(head_of[p] * PEER_D_QUERY, PEER_D_QUERY)
            tl = p % n_lb
            qh = q_s[tl * LANE:(tl + 1) * LANE, pl.ds(off, PEER_D_QUERY)]
            msq = jnp.mean(qh * qh, axis=-1, keepdims=True)
            qn = (qh * lax.rsqrt(msq + NORM_EPS)) * qg_ref[:, pl.ds(off, PEER_D_QUERY)]
            s1.append(_dot3_nt(k1_ref[head_of[p]], qn[:, 0:PEER_HALF]))
            s2.append(_dot3_nt(k2_ref[head_of[p]], qn[:, PEER_HALF:PEER_D_QUERY]))
        _top16_sorted([(s1[p], v1_s.at[p]) for p in pieces] + [(s2[p], v2_s.at[p]) for p in pieces])
        for p in pieces:
            row = 0
            for a, nb in enumerate(_CAND_COUNTS):
                cand_s[p, row:row + nb, :] = v1_s[p, a:a + 1, :] + v2_s[p, 0:nb, :]
                row += nb
        cand = [cand_s[p] for p in pieces]
        m0 = [jnp.max(cand[p], axis=0, keepdims=True) for p in pieces]
        m = list(m0)
        z = [jnp.zeros((1, LANE), F32) for p in pieces]
        for k in range(PEER_TOPK):
            for p in pieces:
                if k:
                    m[p] = jnp.max(cand[p], axis=0, keepdims=True)
                z[p] = z[p] + jnp.exp(m[p] - m0[p])
                cand[p] = jnp.where(cand[p] == m[p], NEG_INF, cand[p])
        for p in pieces:
            h = head_of[p]
            tl = p % n_lb
            tok = slice(tl * LANE, (tl + 1) * LANE)
            tau = m[p]
            rank2 = jnp.full(s2[p].shape, float(PEER_TOPK), F32)
            for b in reversed(range(PEER_TOPK)):
                rank2 = jnp.where(s2[p] >= v2_s[p, b:b + 1, :], float(b), rank2)
            cnt = jnp.zeros(s1[p].shape, F32)
            for b in range(PEER_TOPK):
                cnt = jnp.where(s1[p] + v2_s[p, b:b + 1, :] >= tau, float(b + 1), cnt)
            l_ref[h, :, tok] = cnt
            e1_ref[h, :, tok] = jnp.exp(s1[p] - v1_s[p, 0:1, :]) * (1.0 / z[p])
            r2_ref[h, tl] = pltpu.bitcast(rank2.astype(BF16), U32)
            e2_ref[h, tl] = pltpu.bitcast(jnp.exp(s2[p] - v2_s[p, 0:1, :]).astype(BF16), U32)
        return carry

    lax.fori_loop(0, PEER_HEADS // POST_HEAD_GROUP, head, 0)


def _post(y_lru, y_rwkv, x, wo_a, wo_b, g2, wq, qg, k1, k2):
    n = x.shape[0]
    tm = POST_ROWS
    c2 = lambda i: (0, 0)
    c3 = lambda i: (0, 0, 0)
    hk = (PEER_HEADS, PEER_N_KEYS)
    tok = pl.BlockSpec(hk + (tm,), lambda i: (0, 0, i))
    tok_packed = pl.BlockSpec((PEER_HEADS, tm // LANE, PEER_N_KEYS // 2, LANE),
                              lambda i: (0, i, 0, 0))
    return pl.pallas_call(
        _post_body,
        grid=(n // tm,),
        in_specs=[
            pl.BlockSpec((tm, D_LRU), lambda i: (i, 0)),
            pl.BlockSpec((tm, D_RWKV), lambda i: (i, 0)),
            pl.BlockSpec((tm, D_MODEL), lambda i: (i, 0)),
            pl.BlockSpec((D_LRU, D_MODEL), c2),
            pl.BlockSpec((D_RWKV, D_MODEL), c2),
            pl.BlockSpec((1, D_MODEL), c2),
            pl.BlockSpec((D_MODEL, PEER_HEADS * PEER_D_QUERY), c2),
            pl.BlockSpec((1, PEER_HEADS * PEER_D_QUERY), c2),
            pl.BlockSpec(hk + (PEER_HALF,), c3),
            pl.BlockSpec(hk + (PEER_HALF,), c3),
        ],
        out_specs=[
            pl.BlockSpec((tm, D_MODEL), lambda i: (i, 0)),
            pl.BlockSpec((D_MODEL // 2, tm), lambda i: (0, i)),
            tok, tok, tok_packed, tok_packed,
        ],
        out_shape=[
            jax.ShapeDtypeStruct((n, D_MODEL), F32),
            jax.ShapeDtypeStruct((D_MODEL // 2, n), U32),
            jax.ShapeDtypeStruct(hk + (n,), F32),
            jax.ShapeDtypeStruct(hk + (n,), F32),
            jax.ShapeDtypeStruct((PEER_HEADS, n // LANE, PEER_N_KEYS // 2, LANE), U32),
            jax.ShapeDtypeStruct((PEER_HEADS, n // LANE, PEER_N_KEYS // 2, LANE), U32),
        ],
        scratch_shapes=[
            pltpu.VMEM((tm, PEER_HEADS * PEER_D_QUERY), F32),
            pltpu.VMEM((POST_HEAD_GROUP * tm // LANE, PEER_TOPK, LANE), F32),
            pltpu.VMEM((POST_HEAD_GROUP * tm // LANE, PEER_TOPK, LANE), F32),
            pltpu.VMEM((POST_HEAD_GROUP * tm // LANE, _CAND_ROWS, LANE), F32),
        ],
        compiler_params=pltpu.CompilerParams(
            dimension_semantics=("arbitrary",), vmem_limit_bytes=VMEM_LIMIT),
        name="post",
    )(y_lru, y_rwkv, x, wo_a, wo_b, g2, wq, qg, k1, k2)


def _row_bcast_bf16(rows8, ii):
    packed = jnp.broadcast_to(rows8[ii:ii + 1, :], (16, LANE)).astype(BF16)
    return jnp.tile(packed, (PEER_N_KEYS // 16, 1))


EXP_TILES = N_EXPERTS // (EXP_I * PEER_N_KEYS)
EXP_LAG = 2
EXP_MROWS = 128


def _experts_body(xt_ref, l_ref, e1_ref, r2_ref, e2_ref, u_ref, vt_ref, h1_ref, fg_ref, o_ref,
                  acc_ref, ht0_ref, ht1_ref, pt0_ref, pt1_ref, lrow_ref, erow_ref):
    s = pl.program_id(0)
    tn = EXP_TOKENS
    half = tn // 2

    @pl.when(s == 0)
    def _():
        for ref in (acc_ref, ht0_ref, ht1_ref, pt0_ref, pt1_ref):
            ref[...] = jnp.zeros_like(ref)

    e_gate = (s + EXP_TILES - 1) % EXP_TILES
    e_out = (s + EXP_TILES - 2) % EXP_TILES
    i0 = pl.multiple_of(e_gate * EXP_I, EXP_I)

    for h in range(PEER_HEADS):
        lrow_ref[h] = l_ref[h, pl.ds(i0, EXP_I), :]
        erow_ref[h] = e1_ref[h, pl.ds(i0, EXP_I), :]

    def stage(ht_w, ht_r, pt_w, pt_r):
        def hidden_job(mi, hf):
            rs = slice(mi * EXP_MROWS, (mi + 1) * EXP_MROWS)
            hl = slice(hf * half, (hf + 1) * half)
            rp = slice(mi * EXP_MROWS // 2, (mi + 1) * EXP_MROWS // 2)
            ht_w[rs, hl] = jnp.dot(pltpu.bitcast(u_ref[rp, :], BF16),
                                   pltpu.bitcast(xt_ref[:, hl], BF16), preferred_element_type=F32)

        def output_job(mi, hf):
            rs = slice(mi * EXP_MROWS, (mi + 1) * EXP_MROWS)
            hl = slice(hf * half, (hf + 1) * half)
            rp = slice(mi * EXP_MROWS // 2, (mi + 1) * EXP_MROWS // 2)
            acc_ref[rs, hl] += jnp.dot(pltpu.bitcast(vt_ref[rp, :], BF16),
                                       pltpu.bitcast(pt_r[:, hl], BF16), preferred_element_type=F32)

        def gate_piece(lb, ii):
            lanes = slice(lb * LANE, (lb + 1) * LANE)
            rows = slice(ii * PEER_N_KEYS, (ii + 1) * PEER_N_KEYS)
            rows_packed = slice(ii * PEER_N_KEYS // 2, (ii + 1) * PEER_N_KEYS // 2)
            gate = jnp.zeros((PEER_N_KEYS, LANE), BF16)
            for h in range(PEER_HEADS):
                lrow = _row_bcast_bf16(lrow_ref[h, :, lanes], ii)
                erow = _row_bcast_bf16(erow_ref[h, :, lanes], ii)
                sel = pltpu.bitcast(r2_ref[h, lb], BF16) < lrow
                val = pltpu.bitcast(e2_ref[h, lb], BF16) * erow
                gate = gate + jnp.where(sel, val, jnp.zeros_like(val))
            pt = gate * _gelu_erf(ht_r[rows, lanes]).astype(BF16)
            pt_w[rows_packed, lanes] = pltpu.bitcast(pt, U32)

        nm = (EXP_I * PEER_N_KEYS) // EXP_MROWS
        mxu_jobs = ([functools.partial(hidden_job, mi, hf) for hf in range(2) for mi in range(nm)]
                    + [functools.partial(output_job, mi, hf) for hf in range(2)
                       for mi in range(D_MODEL // EXP_MROWS)])
        pieces = [functools.partial(gate_piece, lb, ii)
                  for lb in range(tn // LANE) for ii in range(EXP_I)]
        per_job = -(-len(pieces) // len(mxu_jobs))
        for j, job in enumerate(mxu_jobs):
            job()
            for piece in pieces[j * per_job:(j + 1) * per_job]:
                piece()

    @pl.when(s % 2 == 0)
    def _():
        stage(ht0_ref, ht1_ref, pt0_ref, pt1_ref)

    @pl.when(s % 2 == 1)
    def _():
        stage(ht1_ref, ht0_ref, pt1_ref, pt0_ref)

    @pl.when(jnp.logical_and(e_out == EXP_TILES - 1, s >= EXP_LAG))
    def _():
        h2 = h1_ref[...] + acc_ref[...].T
        ms = jnp.mean(h2 * h2, axis=-1, keepdims=True)
        o_ref[...] = (h2 * lax.rsqrt(ms + NORM_EPS)) * fg_ref[...]
        acc_ref[...] = jnp.zeros_like(acc_ref)


def _experts(xt, lcnt, e1, r2, e2, u_packed, vt_packed, h1, fgain):
    n = h1.shape[0]
    tn = EXP_TOKENS
    te = EXP_I * PEER_N_KEYS
    nb = n // tn
    hk = (PEER_HEADS, PEER_N_KEYS)

    def tok_block(lag):
        return lambda s: jnp.clip((s - lag) // EXP_TILES, 0, nb - 1)

    def tile(lag):
        return lambda s: (s + EXP_TILES - lag) % EXP_TILES

    tb0, tb1, tb2 = tok_block(0), tok_block(1), tok_block(2)
    tok = pl.BlockSpec(hk + (tn,), lambda s: (0, 0, tb1(s)))
    tok_packed = pl.BlockSpec((PEER_HEADS, tn // LANE, PEER_N_KEYS // 2, LANE),
                              lambda s: (0, tb1(s), 0, 0))
    return pl.pallas_call(
        _experts_body,
        grid=(nb * EXP_TILES + EXP_LAG,),
        in_specs=[
            pl.BlockSpec((D_MODEL // 2, tn), lambda s: (0, tb0(s))),
            tok, tok, tok_packed, tok_packed,
            pl.BlockSpec((te // 2, D_MODEL), lambda s: (tile(0)(s), 0)),
            pl.BlockSpec((D_MODEL // 2, te), lambda s: (0, tile(2)(s))),
            pl.BlockSpec((tn, D_MODEL), lambda s: (tb2(s), 0)),
            pl.BlockSpec((1, D_MODEL), lambda s: (0, 0)),
        ],
        out_specs=pl.BlockSpec((tn, D_MODEL), lambda s: (tb2(s), 0)),
        out_shape=jax.ShapeDtypeStruct((n, D_MODEL), F32),
        scratch_shapes=[
            pltpu.VMEM((D_MODEL, tn), F32),
            pltpu.VMEM((te, tn), F32),
            pltpu.VMEM((te, tn), F32),
            pltpu.VMEM((te // 2, tn), U32),
            pltpu.VMEM((te // 2, tn), U32),
            pltpu.VMEM((PEER_HEADS, EXP_I, tn), F32),
            pltpu.VMEM((PEER_HEADS, EXP_I, tn), F32),
        ],
        compiler_params=pltpu.CompilerParams(
            dimension_semantics=("arbitrary",), vmem_limit_bytes=VMEM_LIMIT),
        name="experts",
    )(xt, lcnt, e1, r2, e2, u_packed, vt_packed, h1, fgain)


PACK_ROWS = 512


def _pack_rows_body(w_ref, o_ref):
    o_ref[...] = pltpu.bitcast(w_ref[...].astype(BF16), U32)


def _pack_rows(w):
    r, c = w.shape
    return pl.pallas_call(
        _pack_rows_body,
        grid=(r // PACK_ROWS,),
        in_specs=[pl.BlockSpec((PACK_ROWS, c), lambda i: (i, 0))],
        out_specs=pl.BlockSpec((PACK_ROWS // 2, c), lambda i: (i, 0)),
        out_shape=jax.ShapeDtypeStruct((r // 2, c), U32),
        compiler_params=pltpu.CompilerParams(
            dimension_semantics=("arbitrary",), vmem_limit_bytes=VMEM_LIMIT),
        name="pack_rows",
    )(w)


def _pack_transposed_body(w_ref, o_ref):
    o_ref[...] = pltpu.bitcast(w_ref[...].T.astype(BF16), U32)


def _pack_transposed(w):
    r, c = w.shape
    return pl.pallas_call(
        _pack_transposed_body,
        grid=(r // PACK_ROWS,),
        in_specs=[pl.BlockSpec((PACK_ROWS, c), lambda i: (i, 0))],
        out_specs=pl.BlockSpec((c // 2, PACK_ROWS), lambda i: (0, i)),
        out_shape=jax.ShapeDtypeStruct((c // 2, r), U32),
        compiler_params=pltpu.CompilerParams(
            dimension_semantics=("arbitrary",), vmem_limit_bytes=VMEM_LIMIT),
        name="pack_transposed",
    )(w)


def _block_diag(w):
    nh, d, _ = w.shape
    eye = jnp.eye(nh, dtype=w.dtype)
    return (eye[:, None, :, None] * w[:, :, None, :]).reshape(nh * d, nh * d)


def kernel(x, meta_tokens, norm1_gain, w_in, conv_w, conv_b, lru_gate_a_w, lru_gate_a_b, lru_gate_x_w, lru_gate_x_b, lru_lambda, lru_out_gain, rwkv_shift_mu, rwkv_w0, rwkv_w2, rwkv_a0, rwkv_a2, rwkv_g2, rwkv_k_k, rwkv_k_a, rwkv_r_k, rwkv_gn_w, rwkv_gn_b, w_out, norm2_gain, peer_w_query, peer_q_gain, peer_sub_keys, peer_u, peer_v, final_norm_gain):
    bsz, seq, _ = x.shape
    row = lambda v: v.reshape(1, -1).astype(F32)

    w_in_bf = w_in[0].astype(BF16)
    w_lru, w_rwkv = w_in_bf[:, :2 * D_LRU], w_in_bf[:, 2 * D_LRU:]
    wa_bd = _block_diag(lru_gate_a_w[0]).astype(BF16)
    wx_bd = _block_diag(lru_gate_x_w[0]).astype(BF16)
    zeros_l = jnp.zeros((D_DECAY_LORA, D_RWKV), F32)
    lora_bd = jnp.concatenate(
        [jnp.concatenate([rwkv_w2[0], zeros_l], axis=1),
         jnp.concatenate([zeros_l, rwkv_a2[0]], axis=1)], axis=0).astype(BF16)
    ones_bd = _block_diag(jnp.ones((RWKV_HEADS, RWKV_HEAD_DIM, RWKV_HEAD_DIM), F32)).astype(BF16)
    tri = jnp.tril(jnp.ones((CHUNK, CHUNK), F32)).astype(BF16)
    wo_bf = w_out[0].astype(BF16)
    wq_bf = peer_w_query[0].astype(BF16)
    u_packed = _pack_rows(peer_u[0])
    vt_packed = _pack_transposed(peer_v[0])

    head = jnp.concatenate(
        [jnp.zeros((FRONT_PAD, D_MODEL), x.dtype), meta_tokens.astype(x.dtype)], axis=0)
    h0 = jnp.concatenate([jnp.broadcast_to(head[None], (bsz, MIX_BLOCK, D_MODEL)), x], axis=1)
    t_pad = seq + MIX_BLOCK
    p_lru, p_rwkv = _inproj(h0.reshape(bsz * t_pad, D_MODEL), row(norm1_gain[0]), w_lru, w_rwkv)
    y_lru = _lru(p_lru.reshape(bsz, t_pad, 2 * D_LRU), conv_w[0], row(conv_b[0]), wa_bd,
                 row(lru_gate_a_b[0]), wx_bd, row(lru_gate_x_b[0]), row(lru_lambda[0]),
                 row(lru_out_gain[0]), seq)
    y_rwkv = _rwkv(p_rwkv.reshape(bsz, t_pad, D_RWKV_IN), row(rwkv_shift_mu[0]), row(rwkv_w0[0]),
                   lora_bd, row(rwkv_a0[0]), rwkv_g2[0].astype(BF16), row(rwkv_k_k[0]),
                   row(rwkv_k_a[0]), row(rwkv_r_k[0]), row(rwkv_gn_w[0]), row(rwkv_gn_b[0]),
                   ones_bd, tri, seq)

    n = bsz * seq
    h1, xt, lcnt, e1, r2, e2 = _post(
        y_lru.reshape(n, D_LRU), y_rwkv.reshape(n, D_RWKV), x.reshape(n, D_MODEL),
        wo_bf[:D_LRU], wo_bf[D_LRU:], row(norm2_gain[0]), wq_bf, row(peer_q_gain[0]),
        peer_sub_keys[0, :, 0], peer_sub_keys[0, :, 1])
    out = _experts(xt, lcnt, e1, r2, e2, u_packed, vt_packed, h1, row(final_norm_gain))
    return out.reshape(bsz, seq, D_MODEL)
```

```python
import functools

import jax
import jax.numpy as jnp
from jax import lax
from jax.experimental import pallas as pl
from jax.experimental.pallas import tpu as pltpu

F32 = jnp.float32
BF16 = jnp.bfloat16
U32 = jnp.uint32

D_MODEL = 1024
N_META = 16
NORM_EPS = 1e-6
D_LRU = 512
D_RWKV = 512
LRU_C = 8.0
RWKV_HEAD_DIM = 64
RWKV_HEADS = D_RWKV // RWKV_HEAD_DIM
D_DECAY_LORA = 64
D_AAA_LORA = 64
D_GATE_LORA = 128
RWKV_GN_EPS = 64e-5
D_RWKV_IN = 3 * D_RWKV + D_DECAY_LORA + D_AAA_LORA + D_GATE_LORA
PEER_HEADS = 8
PEER_N_KEYS = 128
PEER_D_QUERY = 256
PEER_HALF = PEER_D_QUERY // 2
PEER_TOPK = 16
N_EXPERTS = PEER_N_KEYS * PEER_N_KEYS

MIX_BLOCK = 256
FRONT_PAD = MIX_BLOCK - N_META
CHUNK = 64
POST_ROWS = 512
POST_HEAD_GROUP = 2
EXP_TOKENS = 512
EXP_I = 8
LANE = 128
NEG_INF = float("-inf")
VMEM_LIMIT = 52 * 1024 * 1024


def _dot(a, b):
    return jnp.dot(a.astype(BF16), b.astype(BF16), preferred_element_type=F32)


def _dot_nt(a, b):
    return lax.dot_general(a.astype(BF16), b.astype(BF16), (((1,), (1,)), ((), ())),
                           preferred_element_type=F32)


def _split2(x):
    hi = x.astype(BF16)
    lo = (x - hi.astype(F32)).astype(BF16)
    return hi, lo


def _split3(x):
    hi = x.astype(BF16)
    r1 = x - hi.astype(F32)
    mid = r1.astype(BF16)
    lo = (r1 - mid.astype(F32)).astype(BF16)
    return hi, mid, lo


def _dot3(a, b):
    ah, al = _split2(a)
    bh, bl = _split2(b)
    return (jnp.dot(ah, bh, preferred_element_type=F32)
            + jnp.dot(al, bh, preferred_element_type=F32)
            + jnp.dot(ah, bl, preferred_element_type=F32))


def _dot3_nt(a, b):
    ah, al = _split2(a)
    bh, bl = _split2(b)
    dn = (((1,), (1,)), ((), ()))
    return (lax.dot_general(ah, bh, dn, preferred_element_type=F32)
            + lax.dot_general(al, bh, dn, preferred_element_type=F32)
            + lax.dot_general(ah, bl, dn, preferred_element_type=F32))


def _group_sum(x, ones_bd):
    hi, lo = _split2(x)
    return (jnp.dot(hi, ones_bd, preferred_element_type=F32)
            + jnp.dot(lo, ones_bd, preferred_element_type=F32))


def _softplus(x):
    return jnp.maximum(x, 0.0) + jnp.log1p(jnp.exp(-jnp.abs(x)))


def _sigmoid(x):
    return 1.0 / (1.0 + jnp.exp(-x))


def _gelu_tanh(x):
    return 0.5 * x * (1.0 + jnp.tanh(0.7978845608028654 * (x + 0.044715 * (x * x * x))))


def _gelu_erf(x):
    return 0.5 * x * (1.0 + lax.erf(x * 0.7071067811865476))


def _inproj_body(head_ref, x_ref, g_ref, wl_ref, wr_ref, ol_ref, or_ref):
    x = jnp.where(pl.program_id(1) == 0, head_ref[...], x_ref[0])
    ms = jnp.mean(x * x, axis=-1, keepdims=True)
    xn = ((x * lax.rsqrt(ms + NORM_EPS)) * g_ref[...]).astype(BF16)
    ol_ref[0] = jnp.dot(xn, wl_ref[...], preferred_element_type=F32)
    or_ref[0] = jnp.dot(xn, wr_ref[...], preferred_element_type=F32)


def _inproj(head, x, gain, w_lru, w_rwkv):
    bsz, seq, _ = x.shape
    tm = MIX_BLOCK
    nt = seq // tm + 1
    const = lambda b, t: (0, 0)
    return pl.pallas_call(
        _inproj_body,
        grid=(bsz, nt),
        in_specs=[
            pl.BlockSpec((tm, D_MODEL), const),
            pl.BlockSpec((1, tm, D_MODEL), lambda b, t: (b, jnp.maximum(t - 1, 0), 0)),
            pl.BlockSpec((1, D_MODEL), const),
            pl.BlockSpec((D_MODEL, 2 * D_LRU), const),
            pl.BlockSpec((D_MODEL, D_RWKV_IN), const),
        ],
        out_specs=[
            pl.BlockSpec((1, tm, 2 * D_LRU), lambda b, t: (b, t, 0)),
            pl.BlockSpec((1, tm, D_RWKV_IN), lambda b, t: (b, t, 0)),
        ],
        out_shape=[
            jax.ShapeDtypeStruct((bsz, nt * tm, 2 * D_LRU), F32),
            jax.ShapeDtypeStruct((bsz, nt * tm, D_RWKV_IN), F32),
        ],
        compiler_params=pltpu.CompilerParams(
            dimension_semantics=("arbitrary", "arbitrary"), vmem_limit_bytes=VMEM_LIMIT),
        name="inproj",
    )(head, x, gain, w_lru, w_rwkv)


def _lru_body(p_ref, cw_ref, cb_ref, wa_ref, ba_ref, wx_ref, bx_ref, lam_ref, gain_ref, u_ref, v_ref,
              o_ref, up_ref, vtp_ref, xext_ref, h_ref):
    t = pl.program_id(1)
    tb = MIX_BLOCK

    up_ref[...] = pltpu.bitcast(u_ref[...].astype(BF16), U32)
    vtp_ref[...] = pltpu.bitcast(v_ref[...].T.astype(BF16), U32)

    @pl.when(t == 0)
    def _():
        xext_ref[0:8, :] = jnp.zeros((8, D_LRU), F32)
        h_ref[...] = jnp.zeros_like(h_ref)

    x = p_ref[0, :, 0:D_LRU]
    gate = p_ref[0, :, D_LRU:2 * D_LRU]
    xext_ref[8:8 + tb, :] = x
    cw = cw_ref[...]
    xc = (cb_ref[...] + cw[3:4, :] * x
          + cw[2:3, :] * xext_ref[7:7 + tb, :]
          + cw[1:2, :] * xext_ref[6:6 + tb, :]
          + cw[0:1, :] * xext_ref[5:5 + tb, :])
    xext_ref[0:8, :] = x[tb - 8:tb, :]

    xcb = xc.astype(BF16)
    r = _sigmoid(jnp.dot(xcb, wa_ref[...], preferred_element_type=F32) + ba_ref[...])
    i = _sigmoid(jnp.dot(xcb, wx_ref[...], preferred_element_type=F32) + bx_ref[...])
    log_a = (-LRU_C) * r * _softplus(-lam_ref[...])
    a = jnp.exp(log_a)
    th = jnp.tanh(log_a)
    one_minus_a2 = (-2.0 * th) / (1.0 - th)
    b = jnp.sqrt(one_minus_a2) * (i * xc)
    rows = lax.broadcasted_iota(jnp.int32, (tb, D_LRU), 0)
    b = jnp.where(rows + t * tb >= FRONT_PAD, b, 0.0)

    d = 1
    while d < tb:
        keep = rows >= d
        a_sh = jnp.where(keep, pltpu.roll(a, d, axis=0), 1.0)
        b_sh = jnp.where(keep, pltpu.roll(b, d, axis=0), 0.0)
        b = a * b_sh + b
        a = a * a_sh
        d *= 2
    h = a * h_ref[...] + b
    h_ref[...] = h[tb - 1:tb, :]

    y = h * _gelu_tanh(gate)
    ms = jnp.mean(y * y, axis=-1, keepdims=True)
    o_ref[0] = ((y * lax.rsqrt(ms + NORM_EPS)) * gain_ref[...]).astype(BF16)


def _lru(p_lru, conv_w, conv_b, wa_bd, ba, wx_bd, bx, lam, gain, u_table, v_table, seq):
    bsz, t_pad, _ = p_lru.shape
    nt = t_pad // MIX_BLOCK
    n_exp, d = u_table.shape
    pack_steps = 1 << ((bsz * nt).bit_length() - 1)
    pack_rows = n_exp // pack_steps
    pack_idx = lambda b, t: jnp.minimum(b * nt + t, pack_steps - 1)
    c2 = lambda b, t: (0, 0)
    vec = pl.BlockSpec((1, D_LRU), c2)
    mat = pl.BlockSpec((D_LRU, D_LRU), c2)
    table = pl.BlockSpec((pack_rows, d), lambda b, t: (pack_idx(b, t), 0))
    return pl.pallas_call(
        _lru_body,
        grid=(bsz, nt),
        in_specs=[
            pl.BlockSpec((1, MIX_BLOCK, 2 * D_LRU), lambda b, t: (b, t, 0)),
            pl.BlockSpec((4, D_LRU), c2), vec, mat, vec, mat, vec, vec, vec, table, table,
        ],
        out_specs=[
            pl.BlockSpec((1, MIX_BLOCK, D_LRU), lambda b, t: (b, jnp.maximum(t - 1, 0), 0)),
            pl.BlockSpec((pack_rows // 2, d), lambda b, t: (pack_idx(b, t), 0)),
            pl.BlockSpec((d // 2, pack_rows), lambda b, t: (0, pack_idx(b, t))),
        ],
        out_shape=[
            jax.ShapeDtypeStruct((bsz, seq, D_LRU), BF16),
            jax.ShapeDtypeStruct((n_exp // 2, d), U32),
            jax.ShapeDtypeStruct((d // 2, n_exp), U32),
        ],
        scratch_shapes=[pltpu.VMEM((MIX_BLOCK + 8, D_LRU), F32), pltpu.VMEM((1, D_LRU), F32)],
        compiler_params=pltpu.CompilerParams(
            dimension_semantics=("arbitrary", "arbitrary"), vmem_limit_bytes=VMEM_LIMIT),
        name="lru",
    )(p_lru, conv_w, conv_b, wa_bd, ba, wx_bd, bx, lam, gain, u_table, v_table)


def _rwkv_body(p_ref, mu_ref, w0_ref, lora_ref, a0_ref, g2_ref, kk_ref, ka_ref, rk_ref,
               gnw_ref, gnb_ref, ones_ref, tri_ref, o_ref, carry_ref, s_ref):
    t = pl.program_id(1)
    tb = MIX_BLOCK
    hd = RWKV_HEAD_DIM

    @pl.when(t == 0)
    def _():
        carry_ref[...] = jnp.zeros_like(carry_ref)
        s_ref[...] = jnp.zeros_like(s_ref)

    p = p_ref[0]
    rows = lax.broadcasted_iota(jnp.int32, (tb, D_RWKV_IN), 0)
    prev = jnp.where(rows == 0, carry_ref[...], pltpu.roll(p, 1, axis=0))
    carry_ref[...] = p[tb - 1:tb, :]
    ps = p + (prev - p) * mu_ref[...]

    r = ps[:, 0:D_RWKV]
    k = ps[:, D_RWKV:2 * D_RWKV]
    v = ps[:, 2 * D_RWKV:3 * D_RWKV]
    lo = ps[:, 3 * D_RWKV:3 * D_RWKV + 128]
    gl = ps[:, 3 * D_RWKV + 128:D_RWKV_IN]
    lane = lax.broadcasted_iota(jnp.int32, (tb, 128), 1)
    lo_act = jnp.where(lane < D_DECAY_LORA, jnp.tanh(lo), lo)
    la = _dot(lo_act, lora_ref[...])
    log_w = -_softplus(-(w0_ref[...] + la[:, 0:D_RWKV])) - 0.5
    ld = -jnp.exp(log_w)
    a = _sigmoid(a0_ref[...] + la[:, D_RWKV:2 * D_RWKV])
    g = _dot(_sigmoid(gl), g2_ref[...])

    ones_bd = ones_ref[...]
    kkr = k * kk_ref[...]
    kk = kkr / jnp.maximum(jnp.sqrt(_group_sum(kkr * kkr, ones_bd)), 1e-12)
    k2 = k * (1.0 + (a - 1.0) * ka_ref[...])
    bonus = _group_sum(r * k2 * rk_ref[...], ones_bd) * v
    kka = kk * a

    tri = tri_ref[...]
    ri = lax.broadcasted_iota(jnp.int32, (CHUNK, CHUNK), 0)
    ci = lax.broadcasted_iota(jnp.int32, (CHUNK, CHUNK), 1)
    strict = ri > ci
    incl = ri >= ci
    eye = ri == ci

    nchunk = tb // CHUNK
    at_i, rt_i, v_i, gram, bhat_t, khat_t, wtot_i = [], [], [], [], [], [], []
    for c in range(nchunk):
        sl = slice(c * CHUNK, (c + 1) * CHUNK)
        ld_c = ld[sl]
        hi, mid, lw = _split3(ld_c)
        cum = (jnp.dot(tri, hi, preferred_element_type=F32)
               + jnp.dot(tri, mid, preferred_element_type=F32)
               + jnp.dot(tri, lw, preferred_element_type=F32))
        ltot = cum[CHUNK - 1:CHUNK, :]
        e_inv = jnp.exp(-cum)
        e_end = jnp.exp(ltot - cum)
        at = -kk[sl] * jnp.exp(cum - ld_c)
        rt = r[sl] * jnp.exp(cum)
        bt = kka[sl] * e_inv
        kt = k2[sl] * e_inv
        bhat = kka[sl] * e_end
        khat = k2[sl] * e_end
        wtot = jnp.exp(ltot)
        for h in range(RWKV_HEADS):
            hs = slice(h * hd, (h + 1) * hd)
            at_i.append(at[:, hs])
            rt_i.append(rt[:, hs])
            v_i.append(v[sl, hs])
            wtot_i.append(wtot[:, hs])
            gram.append(_dot_nt(jnp.concatenate([at[:, hs], rt[:, hs]], axis=0),
                                jnp.concatenate([bt[:, hs], kt[:, hs]], axis=0)))
            bhat_t.append(bhat[:, hs].T)
            khat_t.append(khat[:, hs].T)

    rng = range(nchunk * RWKV_HEADS)
    pw = [jnp.where(strict, gram[i][0:CHUNK, 0:CHUNK], 0.0) for i in rng]
    a_ak = [jnp.where(strict, gram[i][0:CHUNK, CHUNK:2 * CHUNK], 0.0) for i in rng]
    a_rb = [jnp.where(incl, gram[i][CHUNK:2 * CHUNK, 0:CHUNK], 0.0) for i in rng]
    a_rk = [jnp.where(incl, gram[i][CHUNK:2 * CHUNK, CHUNK:2 * CHUNK], 0.0) for i in rng]
    akv = [_dot(a_ak[i], v_i[i]) for i in rng]
    uu = [jnp.concatenate([at_i[i], akv[i]], axis=1) for i in rng]
    for level in range(6):
        if level < 5:
            prod = [_dot(pw[i], jnp.concatenate([uu[i], pw[i]], axis=1)) for i in rng]
            pw = [prod[i][:, 2 * hd:3 * hd] for i in rng]
            uu = [uu[i] + prod[i][:, 0:2 * hd] for i in rng]
        else:
            uu = [uu[i] + _dot(pw[i], uu[i]) for i in rng]
    zeros_hd = jnp.zeros((CHUNK, hd), F32)
    tail = [_dot(jnp.concatenate([jnp.concatenate([a_rb[i], a_rk[i]], axis=1),
                                  jnp.concatenate([bhat_t[i], khat_t[i]], axis=1)], axis=0),
                 jnp.concatenate([uu[i], jnp.concatenate([zeros_hd, v_i[i]], axis=1)], axis=0))
            for i in rng]
    ra = [rt_i[i] + tail[i][0:CHUNK, 0:hd] for i in rng]
    ov = [tail[i][0:CHUNK, hd:2 * hd] for i in rng]
    m_i = [tail[i][CHUNK:2 * CHUNK, 0:hd] + jnp.where(eye, wtot_i[i], 0.0) for i in rng]
    sv = [tail[i][CHUNK:2 * CHUNK, hd:2 * hd] for i in rng]

    s_all = s_ref[...]
    s_h = [s_all[:, h * hd:(h + 1) * hd] for h in range(RWKV_HEADS)]
    for c in range(nchunk):
        sl = slice(c * CHUNK, (c + 1) * CHUNK)
        base = c * RWKV_HEADS
        outs = [_dot(ra[base + h], s_h[h]) + ov[base + h] for h in range(RWKV_HEADS)]
        s_h = [_dot3(m_i[base + h], s_h[h]) + sv[base + h] for h in range(RWKV_HEADS)]
        o = jnp.concatenate(outs, axis=1)
        mean = _group_sum(o, ones_bd) * (1.0 / hd)
        dev = o - mean
        var = _group_sum(dev * dev, ones_bd) * (1.0 / hd)
        on = dev * lax.rsqrt(var + RWKV_GN_EPS) * gnw_ref[...] + gnb_ref[...] + bonus[sl]
        o_ref[0, sl, :] = (on * g[sl]).astype(BF16)
    s_ref[...] = jnp.concatenate(s_h, axis=1)


def _rwkv(p_rwkv, mu, w0, lora_bd, a0, g2, k_k, k_a, r_k, gn_w, gn_b, ones_bd, tri, seq):
    bsz, t_pad, _ = p_rwkv.shape
    nt = t_pad // MIX_BLOCK
    c2 = lambda b, t: (0, 0)
    vec = pl.BlockSpec((1, D_RWKV), c2)
    return pl.pallas_call(
        _rwkv_body,
        grid=(bsz, nt),
        in_specs=[
            pl.BlockSpec((1, MIX_BLOCK, D_RWKV_IN), lambda b, t: (b, t, 0)),
            pl.BlockSpec((1, D_RWKV_IN), c2), vec,
            pl.BlockSpec((128, 2 * D_RWKV), c2), vec,
            pl.BlockSpec((D_GATE_LORA, D_RWKV), c2), vec, vec, vec, vec, vec,
            pl.BlockSpec((D_RWKV, D_RWKV), c2),
            pl.BlockSpec((CHUNK, CHUNK), c2),
        ],
        out_specs=pl.BlockSpec((1, MIX_BLOCK, D_RWKV), lambda b, t: (b, jnp.maximum(t - 1, 0), 0)),
        out_shape=jax.ShapeDtypeStruct((bsz, seq, D_RWKV), BF16),
        scratch_shapes=[pltpu.VMEM((1, D_RWKV_IN), F32), pltpu.VMEM((RWKV_HEAD_DIM, D_RWKV), F32)],
        compiler_params=pltpu.CompilerParams(
            dimension_semantics=("arbitrary", "arbitrary"), vmem_limit_bytes=VMEM_LIMIT),
        name="rwkv",
    )(p_rwkv, mu, w0, lora_bd, a0, g2, k_k, k_a, r_k, gn_w, gn_b, ones_bd, tri)


_CAND_COUNTS = tuple(PEER_TOPK // (a + 1) for a in range(PEER_TOPK))
_CAND_ROWS = -(-sum(_CAND_COUNTS) // 8) * 8


def _sort16_network():
    pairs = []

    def merge(lo, n, r):
        step = r * 2
        if step < n:
            merge(lo, n, step)
            merge(lo + r, n, step)
            pairs.extend((i, i + r) for i in range(lo + r, lo + n - r, step))
        else:
            pairs.append((lo, lo + r))

    def sort(lo, n):
        if n > 1:
            sort(lo, n // 2)
            sort(lo + n // 2, n // 2)
            merge(lo, n, 1)

    sort(0, PEER_TOPK)
    return tuple(pairs)


_SORT16 = _sort16_network()


def _top16_sorted(streams):
    all_slabs = []
    for s, _ in streams:
        slabs = [s[8 * k:8 * (k + 1), :] for k in range(PEER_N_KEYS // 8)]
        for i, j in _SORT16:
            slabs[i], slabs[j] = jnp.maximum(slabs[i], slabs[j]), jnp.minimum(slabs[i], slabs[j])
        all_slabs.append(slabs)
    for a in range(PEER_TOPK):
        for slabs, (_, v_ref) in zip(all_slabs, streams):
            m = jnp.max(slabs[0], axis=0, keepdims=True)
            v_ref[a:a + 1, :] = m
            hit = slabs[0] == m
            for k in range(PEER_TOPK - 1 - a):
                slabs[k] = jnp.where(hit, slabs[k + 1], slabs[k])


def _post_body(yl_ref, yr_ref, x_ref, woa_ref, wob_ref, g2_ref, wq_ref, qg_ref, k1_ref, k2_ref,
               h1_ref, xt_ref, l_ref, e1_ref, r2_ref, e2_ref, q_s, v1_s, v2_s, cand_s):
    tm = POST_ROWS
    h1 = (x_ref[...] + jnp.dot(yl_ref[...], woa_ref[...], preferred_element_type=F32)
          + jnp.dot(yr_ref[...], wob_ref[...], preferred_element_type=F32))
    h1_ref[...] = h1
    ms = jnp.mean(h1 * h1, axis=-1, keepdims=True)
    xn = (h1 * lax.rsqrt(ms + NORM_EPS)) * g2_ref[...]
    xt_ref[...] = pltpu.bitcast(xn.T.astype(BF16), U32)
    q_s[...] = jnp.dot(xn.astype(BF16), wq_ref[...], preferred_element_type=F32)
    cand_s[...] = jnp.full(cand_s.shape, NEG_INF, F32)

    n_lb = tm // LANE

    def head(hg, carry):
        pieces = range(POST_HEAD_GROUP * n_lb)
        head_of = [hg * POST_HEAD_GROUP + p // n_lb for p in pieces]
        s1, s2 = [], []
        for p in pieces:
            off = pl.multiple_of(head_of[p] * PEER_D_QUERY, PEER_D_QUERY)
            tl = p % n_lb
            qh = q_s[tl * LANE:(tl + 1) * LANE, pl.ds(off, PEER_D_QUERY)]
            msq = jnp.mean(qh * qh, axis=-1, keepdims=True)
            qn = (qh * lax.rsqrt(msq + NORM_EPS)) * qg_ref[:, pl.ds(off, PEER_D_QUERY)]
            s1.append(_dot3_nt(k1_ref[head_of[p]], qn[:, 0:PEER_HALF]))
            s2.append(_dot3_nt(k2_ref[head_of[p]], qn[:, PEER_HALF:PEER_D_QUERY]))
        _top16_sorted([(s1[p], v1_s.at[p]) for p in pieces] + [(s2[p], v2_s.at[p]) for p in pieces])
        for p in pieces:
            row = 0
            for a, nb in enumerate(_CAND_COUNTS):
                cand_s[p, row:row + nb, :] = v1_s[p, a:a + 1, :] + v2_s[p, 0:nb, :]
                row += nb
        cand = [cand_s[p] for p in pieces]
        m0 = [jnp.max(cand[p], axis=0, keepdims=True) for p in pieces]
        m = list(m0)
        z = [jnp.zeros((1, LANE), F32) for p in pieces]
        for k in range(PEER_TOPK):
            for p in pieces:
                if k:
                    m[p] = jnp.max(cand[p], axis=0, keepdims=True)
                z[p] = z[p] + jnp.exp(m[p] - m0[p])
                cand[p] = jnp.where(cand[p] == m[p], NEG_INF, cand[p])
        for p in pieces:
            h = head_of[p]
            tl = p % n_lb
            tok = slice(tl * LANE, (tl + 1) * LANE)
            tau = m[p]
            rank2 = jnp.full(s2[p].shape, float(PEER_TOPK), F32)
            for b in reversed(range(PEER_TOPK)):
                rank2 = jnp.where(s2[p] >= v2_s[p, b:b + 1, :], float(b), rank2)
            cnt = jnp.zeros(s1[p].shape, F32)
            for b in range(PEER_TOPK):
                cnt = jnp.where(s1[p] + v2_s[p, b:b + 1, :] >= tau, float(b + 1), cnt)
            l_ref[h, :, tok] = cnt
            e1_ref[h, :, tok] = jnp.exp(s1[p] - v1_s[p, 0:1, :]) * (1.0 / z[p])
            r2_ref[h, tl] = pltpu.bitcast(rank2.astype(BF16), U32)
            e2_ref[h, tl] = pltpu.bitcast(jnp.exp(s2[p] - v2_s[p, 0:1, :]).astype(BF16), U32)
        return carry

    lax.fori_loop(0, PEER_HEADS // POST_HEAD_GROUP, head, 0)


def _post(y_lru, y_rwkv, x, wo_a, wo_b, g2, wq, qg, k1, k2):
    n = x.shape[0]
    tm = POST_ROWS
    c2 = lambda i: (0, 0)
    c3 = lambda i: (0, 0, 0)
    hk = (PEER_HEADS, PEER_N_KEYS)
    tok = pl.BlockSpec(hk + (tm,), lambda i: (0, 0, i))
    tok_packed = pl.BlockSpec((PEER_HEADS, tm // LANE, PEER_N_KEYS // 2, LANE),
                              lambda i: (0, i, 0, 0))
    return pl.pallas_call(
        _post_body,
        grid=(n // tm,),
        in_specs=[
            pl.BlockSpec((tm, D_LRU), lambda i: (i, 0)),
            pl.BlockSpec((tm, D_RWKV), lambda i: (i, 0)),
            pl.BlockSpec((tm, D_MODEL), lambda i: (i, 0)),
            pl.BlockSpec((D_LRU, D_MODEL), c2),
            pl.BlockSpec((D_RWKV, D_MODEL), c2),
            pl.BlockSpec((1, D_MODEL), c2),
            pl.BlockSpec((D_MODEL, PEER_HEADS * PEER_D_QUERY), c2),
            pl.BlockSpec((1, PEER_HEADS * PEER_D_QUERY), c2),
            pl.BlockSpec(hk + (PEER_HALF,), c3),
            pl.BlockSpec(hk + (PEER_HALF,), c3),
        ],
        out_specs=[
            pl.BlockSpec((tm, D_MODEL), lambda i: (i, 0)),
            pl.BlockSpec((D_MODEL // 2, tm), lambda i: (0, i)),
            tok, tok, tok_packed, tok_packed,
        ],
        out_shape=[
            jax.ShapeDtypeStruct((n, D_MODEL), F32),
            jax.ShapeDtypeStruct((D_MODEL // 2, n), U32),
            jax.ShapeDtypeStruct(hk + (n,), F32),
            jax.ShapeDtypeStruct(hk + (n,), F32),
            jax.ShapeDtypeStruct((PEER_HEADS, n // LANE, PEER_N_KEYS // 2, LANE), U32),
            jax.ShapeDtypeStruct((PEER_HEADS, n // LANE, PEER_N_KEYS // 2, LANE), U32),
        ],
        scratch_shapes=[
            pltpu.VMEM((tm, PEER_HEADS * PEER_D_QUERY), F32),
            pltpu.VMEM((POST_HEAD_GROUP * tm // LANE, PEER_TOPK, LANE), F32),
            pltpu.VMEM((POST_HEAD_GROUP * tm // LANE, PEER_TOPK, LANE), F32),
            pltpu.VMEM((POST_HEAD_GROUP * tm // LANE, _CAND_ROWS, LANE), F32),
        ],
        compiler_params=pltpu.CompilerParams(
            dimension_semantics=("arbitrary",), vmem_limit_bytes=VMEM_LIMIT),
        name="post",
    )(y_lru, y_rwkv, x, wo_a, wo_b, g2, wq, qg, k1, k2)


def _row_bcast_bf16(rows8, ii):
    packed = jnp.broadcast_to(rows8[ii:ii + 1, :], (16, LANE)).astype(BF16)
    return jnp.tile(packed, (PEER_N_KEYS // 16, 1))


EXP_TILES = N_EXPERTS // (EXP_I * PEER_N_KEYS)
EXP_LAG = 2
EXP_MROWS = 128


def _experts_body(xt_ref, l_ref, e1_ref, r2_ref, e2_ref, u_ref, vt_ref, h1_ref, fg_ref, o_ref,
                  acc_ref, ht0_ref, ht1_ref, pt0_ref, pt1_ref, lrow_ref, erow_ref):
    s = pl.program_id(0)
    tn = EXP_TOKENS
    half = tn // 2

    @pl.when(s == 0)
    def _():
        for ref in (acc_ref, ht0_ref, ht1_ref, pt0_ref, pt1_ref):
            ref[...] = jnp.zeros_like(ref)

    e_gate = (s + EXP_TILES - 1) % EXP_TILES
    e_out = (s + EXP_TILES - 2) % EXP_TILES
    i0 = pl.multiple_of(e_gate * EXP_I, EXP_I)

    for h in range(PEER_HEADS):
        lrow_ref[h] = l_ref[h, pl.ds(i0, EXP_I), :]
        erow_ref[h] = e1_ref[h, pl.ds(i0, EXP_I), :]

    def stage(ht_w, ht_r, pt_w, pt_r):
        def hidden_job(mi, hf):
            rs = slice(mi * EXP_MROWS, (mi + 1) * EXP_MROWS)
            hl = slice(hf * half, (hf + 1) * half)
            rp = slice(mi * EXP_MROWS // 2, (mi + 1) * EXP_MROWS // 2)
            ht_w[rs, hl] = jnp.dot(pltpu.bitcast(u_ref[rp, :], BF16),
                                   pltpu.bitcast(xt_ref[:, hl], BF16), preferred_element_type=F32)

        def output_job(mi, hf):
            rs = slice(mi * EXP_MROWS, (mi + 1) * EXP_MROWS)
            hl = slice(hf * half, (hf + 1) * half)
            rp = slice(mi * EXP_MROWS // 2, (mi + 1) * EXP_MROWS // 2)
            acc_ref[rs, hl] += jnp.dot(pltpu.bitcast(vt_ref[rp, :], BF16),
                                       pltpu.bitcast(pt_r[:, hl], BF16), preferred_element_type=F32)

        def gate_piece(lb, ii):
            lanes = slice(lb * LANE, (lb + 1) * LANE)
            rows = slice(ii * PEER_N_KEYS, (ii + 1) * PEER_N_KEYS)
            rows_packed = slice(ii * PEER_N_KEYS // 2, (ii + 1) * PEER_N_KEYS // 2)
            gate = jnp.zeros((PEER_N_KEYS, LANE), BF16)
            for h in range(PEER_HEADS):
                lrow = _row_bcast_bf16(lrow_ref[h, :, lanes], ii)
                erow = _row_bcast_bf16(erow_ref[h, :, lanes], ii)
                sel = pltpu.bitcast(r2_ref[h, lb], BF16) < lrow
                val = pltpu.bitcast(e2_ref[h, lb], BF16) * erow
                gate = gate + jnp.where(sel, val, jnp.zeros_like(val))
            pt = gate * _gelu_erf(ht_r[rows, lanes]).astype(BF16)
            pt_w[rows_packed, lanes] = pltpu.bitcast(pt, U32)

        nm = (EXP_I * PEER_N_KEYS) // EXP_MROWS
        mxu_jobs = ([functools.partial(hidden_job, mi, hf) for hf in range(2) for mi in range(nm)]
                    + [functools.partial(output_job, mi, hf) for hf in range(2)
                       for mi in range(D_MODEL // EXP_MROWS)])
        pieces = [functools.partial(gate_piece, lb, ii)
                  for lb in range(tn // LANE) for ii in range(EXP_I)]
        per_job = -(-len(pieces) // len(mxu_jobs))
        for j, job in enumerate(mxu_jobs):
            job()
            for piece in pieces[j * per_job:(j + 1) * per_job]:
                piece()

    @pl.when(s % 2 == 0)
    def _():
        stage(ht0_ref, ht1_ref, pt0_ref, pt1_ref)

    @pl.when(s % 2 == 1)
    def _():
        stage(ht1_ref, ht0_ref, pt1_ref, pt0_ref)

    @pl.when(jnp.logical_and(e_out == EXP_TILES - 1, s >= EXP_LAG))
    def _():
        h2 = h1_ref[...] + acc_ref[...].T
        ms = jnp.mean(h2 * h2, axis=-1, keepdims=True)
        o_ref[...] = (h2 * lax.rsqrt(ms + NORM_EPS)) * fg_ref[...]
        acc_ref[...] = jnp.zeros_like(acc_ref)


def _experts(xt, lcnt, e1, r2, e2, u_packed, vt_packed, h1, fgain):
    n = h1.shape[0]
    tn = EXP_TOKENS
    te = EXP_I * PEER_N_KEYS
    nb = n // tn
    hk = (PEER_HEADS, PEER_N_KEYS)

    def tok_block(lag):
        return lambda s: jnp.clip((s - lag) // EXP_TILES, 0, nb - 1)

    def tile(lag):
        return lambda s: (s + EXP_TILES - lag) % EXP_TILES

    tb0, tb1, tb2 = tok_block(0), tok_block(1), tok_block(2)
    tok = pl.BlockSpec(hk + (tn,), lambda s: (0, 0, tb1(s)))
    tok_packed = pl.BlockSpec((PEER_HEADS, tn // LANE, PEER_N_KEYS // 2, LANE),
                              lambda s: (0, tb1(s), 0, 0))
    return pl.pallas_call(
        _experts_body,
        grid=(nb * EXP_TILES + EXP_LAG,),
        in_specs=[
            pl.BlockSpec((D_MODEL // 2, tn), lambda s: (0, tb0(s))),
            tok, tok, tok_packed, tok_packed,
            pl.BlockSpec((te // 2, D_MODEL), lambda s: (tile(0)(s), 0)),
            pl.BlockSpec((D_MODEL // 2, te), lambda s: (0, tile(2)(s))),
            pl.BlockSpec((tn, D_MODEL), lambda s: (tb2(s), 0)),
            pl.BlockSpec((1, D_MODEL), lambda s: (0, 0)),
        ],
        out_specs=pl.BlockSpec((tn, D_MODEL), lambda s: (tb2(s), 0)),
        out_shape=jax.ShapeDtypeStruct((n, D_MODEL), F32),
        scratch_shapes=[
            pltpu.VMEM((D_MODEL, tn), F32),
            pltpu.VMEM((te, tn), F32),
            pltpu.VMEM((te, tn), F32),
            pltpu.VMEM((te // 2, tn), U32),
            pltpu.VMEM((te // 2, tn), U32),
            pltpu.VMEM((PEER_HEADS, EXP_I, tn), F32),
            pltpu.VMEM((PEER_HEADS, EXP_I, tn), F32),
        ],
        compiler_params=pltpu.CompilerParams(
            dimension_semantics=("arbitrary",), vmem_limit_bytes=VMEM_LIMIT),
        name="experts",
    )(xt, lcnt, e1, r2, e2, u_packed, vt_packed, h1, fgain)


def _block_diag(w):
    nh, d, _ = w.shape
    eye = jnp.eye(nh, dtype=w.dtype)
    return (eye[:, None, :, None] * w[:, :, None, :]).reshape(nh * d, nh * d)


def kernel(x, meta_tokens, norm1_gain, w_in, conv_w, conv_b, lru_gate_a_w, lru_gate_a_b, lru_gate_x_w, lru_gate_x_b, lru_lambda, lru_out_gain, rwkv_shift_mu, rwkv_w0, rwkv_w2, rwkv_a0, rwkv_a2, rwkv_g2, rwkv_k_k, rwkv_k_a, rwkv_r_k, rwkv_gn_w, rwkv_gn_b, w_out, norm2_gain, peer_w_query, peer_q_gain, peer_sub_keys, peer_u, peer_v, final_norm_gain):
    bsz, seq, _ = x.shape
    row = lambda v: v.reshape(1, -1).astype(F32)

    w_in_bf = w_in[0].astype(BF16)
    w_lru, w_rwkv = w_in_bf[:, :2 * D_LRU], w_in_bf[:, 2 * D_LRU:]
    wa_bd = _block_diag(lru_gate_a_w[0]).astype(BF16)
    wx_bd = _block_diag(lru_gate_x_w[0]).astype(BF16)
    zeros_l = jnp.zeros((D_DECAY_LORA, D_RWKV), F32)
    lora_bd = jnp.concatenate(
        [jnp.concatenate([rwkv_w2[0], zeros_l], axis=1),
         jnp.concatenate([zeros_l, rwkv_a2[0]], axis=1)], axis=0).astype(BF16)
    ones_bd = _block_diag(jnp.ones((RWKV_HEADS, RWKV_HEAD_DIM, RWKV_HEAD_DIM), F32)).astype(BF16)
    tri = jnp.tril(jnp.ones((CHUNK, CHUNK), F32)).astype(BF16)
    wo_bf = w_out[0].astype(BF16)
    wq_bf = peer_w_query[0].astype(BF16)

    head = jnp.concatenate(
        [jnp.zeros((FRONT_PAD, D_MODEL), x.dtype), meta_tokens.astype(x.dtype)], axis=0)
    p_lru, p_rwkv = _inproj(head, x, row(norm1_gain[0]), w_lru, w_rwkv)
    y_lru, u_packed, vt_packed = _lru(
        p_lru, conv_w[0], row(conv_b[0]), wa_bd, row(lru_gate_a_b[0]), wx_bd,
        row(lru_gate_x_b[0]), row(lru_lambda[0]), row(lru_out_gain[0]), peer_u[0], peer_v[0], seq)
    y_rwkv = _rwkv(p_rwkv, row(rwkv_shift_mu[0]), row(rwkv_w0[0]),
                   lora_bd, row(rwkv_a0[0]), rwkv_g2[0].astype(BF16), row(rwkv_k_k[0]),
                   row(rwkv_k_a[0]), row(rwkv_r_k[0]), row(rwkv_gn_w[0]), row(rwkv_gn_b[0]),
                   ones_bd, tri, seq)

    n = bsz * seq
    h1, xt, lcnt, e1, r2, e2 = _post(
        y_lru.reshape(n, D_LRU), y_rwkv.reshape(n, D_RWKV), x.reshape(n, D_MODEL),
        wo_bf[:D_LRU], wo_bf[D_LRU:], row(norm2_gain[0]), wq_bf, row(peer_q_gain[0]),
        peer_sub_keys[0, :, 0], peer_sub_keys[0, :, 1])
    out = _experts(xt, lcnt, e1, r2, e2, u_packed, vt_packed, h1, row(final_norm_gain))
    return out.reshape(bsz, seq, D_MODEL)
```

```python
import functools

import jax
import jax.numpy as jnp
from jax import lax
from jax.experimental import pallas as pl
from jax.experimental.pallas import tpu as pltpu

F32 = jnp.float32
BF16 = jnp.bfloat16
U32 = jnp.uint32

D_MODEL = 1024
N_META = 16
NORM_EPS = 1e-6
D_LRU = 512
D_RWKV = 512
LRU_C = 8.0
RWKV_HEAD_DIM = 64
RWKV_HEADS = D_RWKV // RWKV_HEAD_DIM
D_DECAY_LORA = 64
D_AAA_LORA = 64
D_GATE_LORA = 128
RWKV_GN_EPS = 64e-5
D_RWKV_IN = 3 * D_RWKV + D_DECAY_LORA + D_AAA_LORA + D_GATE_LORA
PEER_HEADS = 8
PEER_N_KEYS = 128
PEER_D_QUERY = 256
PEER_HALF = PEER_D_QUERY // 2
PEER_TOPK = 16
N_EXPERTS = PEER_N_KEYS * PEER_N_KEYS

MIX_BLOCK = 256
FRONT_PAD = MIX_BLOCK - N_META
CHUNK = 64
POST_ROWS = 512
POST_HEAD_GROUP = 2
EXP_TOKENS = 512
EXP_I = 8
LANE = 128
NEG_INF = float("-inf")
VMEM_LIMIT = 52 * 1024 * 1024


def _dot(a, b):
    return jnp.dot(a.astype(BF16), b.astype(BF16), preferred_element_type=F32)


def _dot_nt(a, b):
    return lax.dot_general(a.astype(BF16), b.astype(BF16), (((1,), (1,)), ((), ())),
                           preferred_element_type=F32)


def _split2(x):
    hi = x.astype(BF16)
    lo = (x - hi.astype(F32)).astype(BF16)
    return hi, lo


def _split3(x):
    hi = x.astype(BF16)
    r1 = x - hi.astype(F32)
    mid = r1.astype(BF16)
    lo = (r1 - mid.astype(F32)).astype(BF16)
    return hi, mid, lo


def _dot3(a, b):
    ah, al = _split2(a)
    bh, bl = _split2(b)
    return (jnp.dot(ah, bh, preferred_element_type=F32)
            + jnp.dot(al, bh, preferred_element_type=F32)
            + jnp.dot(ah, bl, preferred_element_type=F32))


def _dot3_nt(a, b):
    ah, al = _split2(a)
    bh, bl = _split2(b)
    dn = (((1,), (1,)), ((), ()))
    return (lax.dot_general(ah, bh, dn, preferred_element_type=F32)
            + lax.dot_general(al, bh, dn, preferred_element_type=F32)
            + lax.dot_general(ah, bl, dn, preferred_element_type=F32))


def _group_sum(x, ones_bd):
    return jnp.dot(x.astype(BF16), ones_bd, preferred_element_type=F32)


def _softplus(x):
    return jnp.maximum(x, 0.0) + jnp.log1p(jnp.exp(-jnp.abs(x)))


def _sigmoid(x):
    return 1.0 / (1.0 + jnp.exp(-x))


def _gelu_tanh(x):
    return 0.5 * x * (1.0 + jnp.tanh(0.7978845608028654 * (x + 0.044715 * (x * x * x))))


def _gelu_erf(x):
    return 0.5 * x * (1.0 + lax.erf(x * 0.7071067811865476))


def _inproj_body(head_ref, x_ref, g_ref, wl_ref, wr_ref, ol_ref, or_ref):
    x = jnp.where(pl.program_id(1) == 0, head_ref[...], x_ref[0])
    ms = jnp.mean(x * x, axis=-1, keepdims=True)
    xn = ((x * lax.rsqrt(ms + NORM_EPS)) * g_ref[...]).astype(BF16)
    ol_ref[0] = jnp.dot(xn, wl_ref[...], preferred_element_type=F32)
    or_ref[0] = jnp.dot(xn, wr_ref[...], preferred_element_type=F32)


def _inproj(head, x, gain, w_lru, w_rwkv):
    bsz, seq, _ = x.shape
    tm = MIX_BLOCK
    nt = seq // tm + 1
    const = lambda b, t: (0, 0)
    return pl.pallas_call(
        _inproj_body,
        grid=(bsz, nt),
        in_specs=[
            pl.BlockSpec((tm, D_MODEL), const),
            pl.BlockSpec((1, tm, D_MODEL), lambda b, t: (b, jnp.maximum(t - 1, 0), 0)),
            pl.BlockSpec((1, D_MODEL), const),
            pl.BlockSpec((D_MODEL, 2 * D_LRU), const),
            pl.BlockSpec((D_MODEL, D_RWKV_IN), const),
        ],
        out_specs=[
            pl.BlockSpec((1, tm, 2 * D_LRU), lambda b, t: (b, t, 0)),
            pl.BlockSpec((1, tm, D_RWKV_IN), lambda b, t: (b, t, 0)),
        ],
        out_shape=[
            jax.ShapeDtypeStruct((bsz, nt * tm, 2 * D_LRU), F32),
            jax.ShapeDtypeStruct((bsz, nt * tm, D_RWKV_IN), F32),
        ],
        compiler_params=pltpu.CompilerParams(
            dimension_semantics=("arbitrary", "arbitrary"), vmem_limit_bytes=VMEM_LIMIT),
        name="inproj",
    )(head, x, gain, w_lru, w_rwkv)


def _lru_body(p_ref, cw_ref, cb_ref, wa_ref, ba_ref, wx_ref, bx_ref, lam_ref, gain_ref, u_ref, v_ref,
              o_ref, up_ref, vtp_ref, xext_ref, h_ref):
    t = pl.program_id(1)
    tb = MIX_BLOCK

    up_ref[...] = pltpu.bitcast(u_ref[...].astype(BF16), U32)
    vtp_ref[...] = pltpu.bitcast(v_ref[...].T.astype(BF16), U32)

    @pl.when(t == 0)
    def _():
        xext_ref[0:8, :] = jnp.zeros((8, D_LRU), F32)
        h_ref[...] = jnp.zeros_like(h_ref)

    x = p_ref[0, :, 0:D_LRU]
    gate = p_ref[0, :, D_LRU:2 * D_LRU]
    xext_ref[8:8 + tb, :] = x
    cw = cw_ref[...]
    xc = (cb_ref[...] + cw[3:4, :] * x
          + cw[2:3, :] * xext_ref[7:7 + tb, :]
          + cw[1:2, :] * xext_ref[6:6 + tb, :]
          + cw[0:1, :] * xext_ref[5:5 + tb, :])
    xext_ref[0:8, :] = x[tb - 8:tb, :]

    xcb = xc.astype(BF16)
    r = _sigmoid(jnp.dot(xcb, wa_ref[...], preferred_element_type=F32) + ba_ref[...])
    i = _sigmoid(jnp.dot(xcb, wx_ref[...], preferred_element_type=F32) + bx_ref[...])
    log_a = (-LRU_C) * r * _softplus(-lam_ref[...])
    a = jnp.exp(log_a)
    th = jnp.tanh(log_a)
    one_minus_a2 = (-2.0 * th) / (1.0 - th)
    b = jnp.sqrt(one_minus_a2) * (i * xc)
    rows = lax.broadcasted_iota(jnp.int32, (tb, D_LRU), 0)
    b = jnp.where(rows + t * tb >= FRONT_PAD, b, 0.0)

    d = 1
    while d < tb:
        if d % 8:
            keep = rows >= d
            a_sh = jnp.where(keep, pltpu.roll(a, d, axis=0), 1.0)
            b_sh = jnp.where(keep, pltpu.roll(b, d, axis=0), 0.0)
            b = a * b_sh + b
            a = a * a_sh
        else:
            b = jnp.concatenate([b[:d], a[d:] * b[:tb - d] + b[d:]], axis=0)
            a = jnp.concatenate([a[:d], a[d:] * a[:tb - d]], axis=0)
        d *= 2
    h = a * h_ref[...] + b
    h_ref[...] = h[tb - 1:tb, :]

    y = h * _gelu_tanh(gate)
    ms = jnp.mean(y * y, axis=-1, keepdims=True)
    o_ref[0] = ((y * lax.rsqrt(ms + NORM_EPS)) * gain_ref[...]).astype(BF16)


def _lru(p_lru, conv_w, conv_b, wa_bd, ba, wx_bd, bx, lam, gain, u_table, v_table, seq):
    bsz, t_pad, _ = p_lru.shape
    nt = t_pad // MIX_BLOCK
    n_exp, d = u_table.shape
    pack_steps = 1 << ((bsz * nt).bit_length() - 1)
    pack_rows = n_exp // pack_steps
    pack_idx = lambda b, t: jnp.minimum(b * nt + t, pack_steps - 1)
    c2 = lambda b, t: (0, 0)
    vec = pl.BlockSpec((1, D_LRU), c2)
    mat = pl.BlockSpec((D_LRU, D_LRU), c2)
    table = pl.BlockSpec((pack_rows, d), lambda b, t: (pack_idx(b, t), 0))
    return pl.pallas_call(
        _lru_body,
        grid=(bsz, nt),
        in_specs=[
            pl.BlockSpec((1, MIX_BLOCK, 2 * D_LRU), lambda b, t: (b, t, 0)),
            pl.BlockSpec((4, D_LRU), c2), vec, mat, vec, mat, vec, vec, vec, table, table,
        ],
        out_specs=[
            pl.BlockSpec((1, MIX_BLOCK, D_LRU), lambda b, t: (b, jnp.maximum(t - 1, 0), 0)),
            pl.BlockSpec((pack_rows // 2, d), lambda b, t: (pack_idx(b, t), 0)),
            pl.BlockSpec((d // 2, pack_rows), lambda b, t: (0, pack_idx(b, t))),
        ],
        out_shape=[
            jax.ShapeDtypeStruct((bsz, seq, D_LRU), BF16),
            jax.ShapeDtypeStruct((n_exp // 2, d), U32),
            jax.ShapeDtypeStruct((d // 2, n_exp), U32),
        ],
        scratch_shapes=[pltpu.VMEM((MIX_BLOCK + 8, D_LRU), F32), pltpu.VMEM((1, D_LRU), F32)],
        compiler_params=pltpu.CompilerParams(
            dimension_semantics=("arbitrary", "arbitrary"), vmem_limit_bytes=VMEM_LIMIT),
        name="lru",
    )(p_lru, conv_w, conv_b, wa_bd, ba, wx_bd, bx, lam, gain, u_table, v_table)


def _rwkv_body(p_ref, mu_ref, w0_ref, lora_ref, a0_ref, g2_ref, kk_ref, ka_ref, rk_ref,
               gnw_ref, gnb_ref, ones_ref, tri_ref, o_ref, carry_ref, s_ref):
    t = pl.program_id(1)
    tb = MIX_BLOCK
    hd = RWKV_HEAD_DIM

    @pl.when(t == 0)
    def _():
        carry_ref[...] = jnp.zeros_like(carry_ref)
        s_ref[...] = jnp.zeros_like(s_ref)

    p = p_ref[0]
    rows = lax.broadcasted_iota(jnp.int32, (tb, D_RWKV_IN), 0)
    prev = jnp.where(rows == 0, carry_ref[...], pltpu.roll(p, 1, axis=0))
    carry_ref[...] = p[tb - 1:tb, :]
    ps = p + (prev - p) * mu_ref[...]

    r = ps[:, 0:D_RWKV]
    k = ps[:, D_RWKV:2 * D_RWKV]
    v = ps[:, 2 * D_RWKV:3 * D_RWKV]
    lo = ps[:, 3 * D_RWKV:3 * D_RWKV + 128]
    gl = ps[:, 3 * D_RWKV + 128:D_RWKV_IN]
    lane = lax.broadcasted_iota(jnp.int32, (tb, 128), 1)
    lo_act = jnp.where(lane < D_DECAY_LORA, jnp.tanh(lo), lo)
    la = _dot(lo_act, lora_ref[...])
    log_w = -_softplus(-(w0_ref[...] + la[:, 0:D_RWKV])) - 0.5
    ld = -jnp.exp(log_w)
    a = _sigmoid(a0_ref[...] + la[:, D_RWKV:2 * D_RWKV])
    g = _dot(_sigmoid(gl), g2_ref[...])

    ones_bd = ones_ref[...]
    kkr = k * kk_ref[...]
    kk = kkr / jnp.maximum(jnp.sqrt(_group_sum(kkr * kkr, ones_bd)), 1e-12)
    k2 = k * (1.0 + (a - 1.0) * ka_ref[...])
    bonus = _group_sum(r * k2 * rk_ref[...], ones_bd) * v
    kka = kk * a

    tri = tri_ref[...]
    ri = lax.broadcasted_iota(jnp.int32, (CHUNK, CHUNK), 0)
    ci = lax.broadcasted_iota(jnp.int32, (CHUNK, CHUNK), 1)
    strict = ri > ci
    incl = ri >= ci
    eye = ri == ci

    nchunk = tb // CHUNK
    at_i, rt_i, v_i, gram, bhat_t, khat_t, wtot_i = [], [], [], [], [], [], []
    for c in range(nchunk):
        sl = slice(c * CHUNK, (c + 1) * CHUNK)
        ld_c = ld[sl]
        hi, mid, lw = _split3(ld_c)
        cum = (jnp.dot(tri, hi, preferred_element_type=F32)
               + jnp.dot(tri, mid, preferred_element_type=F32)
               + jnp.dot(tri, lw, preferred_element_type=F32))
        ltot = cum[CHUNK - 1:CHUNK, :]
        e_inv = jnp.exp(-cum)
        e_end = jnp.exp(ltot - cum)
        at = -kk[sl] * jnp.exp(cum - ld_c)
        rt = r[sl] * jnp.exp(cum)
        bt = kka[sl] * e_inv
        kt = k2[sl] * e_inv
        bhat = kka[sl] * e_end
        khat = k2[sl] * e_end
        wtot = jnp.exp(ltot)
        for h in range(RWKV_HEADS):
            hs = slice(h * hd, (h + 1) * hd)
            at_i.append(at[:, hs])
            rt_i.append(rt[:, hs])
            v_i.append(v[sl, hs])
            wtot_i.append(wtot[:, hs])
            gram.append(_dot_nt(jnp.concatenate([at[:, hs], rt[:, hs]], axis=0),
                                jnp.concatenate([bt[:, hs], kt[:, hs]], axis=0)))
            bhat_t.append(bhat[:, hs].T)
            khat_t.append(khat[:, hs].T)

    rng = range(nchunk * RWKV_HEADS)
    pw = [jnp.where(strict, gram[i][0:CHUNK, 0:CHUNK], 0.0) for i in rng]
    a_ak = [jnp.where(strict, gram[i][0:CHUNK, CHUNK:2 * CHUNK], 0.0) for i in rng]
    a_rb = [jnp.where(incl, gram[i][CHUNK:2 * CHUNK, 0:CHUNK], 0.0) for i in rng]
    a_rk = [jnp.where(incl, gram[i][CHUNK:2 * CHUNK, CHUNK:2 * CHUNK], 0.0) for i in rng]
    akv = [_dot(a_ak[i], v_i[i]) for i in rng]
    uu = [jnp.concatenate([at_i[i], akv[i]], axis=1) for i in rng]
    for level in range(6):
        if level < 5:
            prod = [_dot(pw[i], jnp.concatenate([uu[i], pw[i]], axis=1)) for i in rng]
            pw = [prod[i][:, 2 * hd:3 * hd] for i in rng]
            uu = [uu[i] + prod[i][:, 0:2 * hd] for i in rng]
        else:
            uu = [uu[i] + _dot(pw[i], uu[i]) for i in rng]
    zeros_hd = jnp.zeros((CHUNK, hd), F32)
    tail = [_dot(jnp.concatenate([jnp.concatenate([a_rb[i], a_rk[i]], axis=1),
                                  jnp.concatenate([bhat_t[i], khat_t[i]], axis=1)], axis=0),
                 jnp.concatenate([uu[i], jnp.concatenate([zeros_hd, v_i[i]], axis=1)], axis=0))
            for i in rng]
    ra = [rt_i[i] + tail[i][0:CHUNK, 0:hd] for i in rng]
    ov = [tail[i][0:CHUNK, hd:2 * hd] for i in rng]
    m_i = [tail[i][CHUNK:2 * CHUNK, 0:hd] + jnp.where(eye, wtot_i[i], 0.0) for i in rng]
    sv = [tail[i][CHUNK:2 * CHUNK, hd:2 * hd] for i in rng]

    s_all = s_ref[...]
    s_h = [s_all[:, h * hd:(h + 1) * hd] for h in range(RWKV_HEADS)]
    for c in range(nchunk):
        sl = slice(c * CHUNK, (c + 1) * CHUNK)
        base = c * RWKV_HEADS
        outs = [_dot(ra[base + h], s_h[h]) + ov[base + h] for h in range(RWKV_HEADS)]
        s_h = [_dot3(m_i[base + h], s_h[h]) + sv[base + h] for h in range(RWKV_HEADS)]
        o = jnp.concatenate(outs, axis=1)
        mean = _group_sum(o, ones_bd) * (1.0 / hd)
        dev = o - mean
        var = _group_sum(dev * dev, ones_bd) * (1.0 / hd)
        on = dev * lax.rsqrt(var + RWKV_GN_EPS) * gnw_ref[...] + gnb_ref[...] + bonus[sl]
        o_ref[0, sl, :] = (on * g[sl]).astype(BF16)
    s_ref[...] = jnp.concatenate(s_h, axis=1)


def _rwkv(p_rwkv, mu, w0, lora_bd, a0, g2, k_k, k_a, r_k, gn_w, gn_b, ones_bd, tri, seq):
    bsz, t_pad, _ = p_rwkv.shape
    nt = t_pad // MIX_BLOCK
    c2 = lambda b, t: (0, 0)
    vec = pl.BlockSpec((1, D_RWKV), c2)
    return pl.pallas_call(
        _rwkv_body,
        grid=(bsz, nt),
        in_specs=[
            pl.BlockSpec((1, MIX_BLOCK, D_RWKV_IN), lambda b, t: (b, t, 0)),
            pl.BlockSpec((1, D_RWKV_IN), c2), vec,
            pl.BlockSpec((128, 2 * D_RWKV), c2), vec,
            pl.BlockSpec((D_GATE_LORA, D_RWKV), c2), vec, vec, vec, vec, vec,
            pl.BlockSpec((D_RWKV, D_RWKV), c2),
            pl.BlockSpec((CHUNK, CHUNK), c2),
        ],
        out_specs=pl.BlockSpec((1, MIX_BLOCK, D_RWKV), lambda b, t: (b, jnp.maximum(t - 1, 0), 0)),
        out_shape=jax.ShapeDtypeStruct((bsz, seq, D_RWKV), BF16),
        scratch_shapes=[pltpu.VMEM((1, D_RWKV_IN), F32), pltpu.VMEM((RWKV_HEAD_DIM, D_RWKV), F32)],
        compiler_params=pltpu.CompilerParams(
            dimension_semantics=("arbitrary", "arbitrary"), vmem_limit_bytes=VMEM_LIMIT),
        name="rwkv",
    )(p_rwkv, mu, w0, lora_bd, a0, g2, k_k, k_a, r_k, gn_w, gn_b, ones_bd, tri)


_CAND_COUNTS = tuple(PEER_TOPK // (a + 1) for a in range(PEER_TOPK))
_CAND_ROWS = -(-sum(_CAND_COUNTS) // 8) * 8


def _sort16_network():
    pairs = []

    def merge(lo, n, r):
        step = r * 2
        if step < n:
            merge(lo, n, step)
            merge(lo + r, n, step)
            pairs.extend((i, i + r) for i in range(lo + r, lo + n - r, step))
        else:
            pairs.append((lo, lo + r))

    def sort(lo, n):
        if n > 1:
            sort(lo, n // 2)
            sort(lo + n // 2, n // 2)
            merge(lo, n, 1)

    sort(0, PEER_TOPK)
    return tuple(pairs)


_SORT16 = _sort16_network()


def _top16_sorted(streams):
    all_slabs = []
    for s, _ in streams:
        slabs = [s[8 * k:8 * (k + 1), :] for k in range(PEER_N_KEYS // 8)]
        for i, j in _SORT16:
            slabs[i], slabs[j] = jnp.maximum(slabs[i], slabs[j]), jnp.minimum(slabs[i], slabs[j])
        all_slabs.append(slabs)
    for a in range(PEER_TOPK):
        for slabs, (_, v_ref) in zip(all_slabs, streams):
            m = jnp.max(slabs[0], axis=0, keepdims=True)
            v_ref[a:a + 1, :] = m
            hit = slabs[0] == m
            for k in range(PEER_TOPK - 1 - a):
                slabs[k] = jnp.where(hit, slabs[k + 1], slabs[k])


def _post_body(yl_ref, yr_ref, x_ref, woa_ref, wob_ref, g2_ref, wq_ref, qg_ref, k1_ref, k2_ref,
               h1_ref, xt_ref, l_ref, e1_ref, r2_ref, e2_ref, q_s, v1_s, v2_s, cand_s):
    tm = POST_ROWS
    h1 = (x_ref[...] + jnp.dot(yl_ref[...], woa_ref[...], preferred_element_type=F32)
          + jnp.dot(yr_ref[...], wob_ref[...], preferred_element_type=F32))
    h1_ref[...] = h1
    ms = jnp.mean(h1 * h1, axis=-1, keepdims=True)
    xn = (h1 * lax.rsqrt(ms + NORM_EPS)) * g2_ref[...]
    xt_ref[...] = pltpu.bitcast(xn.T.astype(BF16), U32)
    q_s[...] = jnp.dot(xn.astype(BF16), wq_ref[...], preferred_element_type=F32)
    cand_s[...] = jnp.full(cand_s.shape, NEG_INF, F32)

    n_lb = tm // LANE

    def head(hg, carry):
        pieces = range(POST_HEAD_GROUP * n_lb)
        head_of = [hg * POST_HEAD_GROUP + p // n_lb for p in pieces]
        s1, s2 = [], []
        for p in pieces:
            off = pl.multiple_of(head_of[p] * PEER_D_QUERY, PEER_D_QUERY)
            tl = p % n_lb
            qh = q_s[tl * LANE:(tl + 1) * LANE, pl.ds(off, PEER_D_QUERY)]
            msq = jnp.mean(qh * qh, axis=-1, keepdims=True)
            qn = (qh * lax.rsqrt(msq + NORM_EPS)) * qg_ref[:, pl.ds(off, PEER_D_QUERY)]
            s1.append(_dot3_nt(k1_ref[head_of[p]], qn[:, 0:PEER_HALF]))
            s2.append(_dot3_nt(k2_ref[head_of[p]], qn[:, PEER_HALF:PEER_D_QUERY]))
        _top16_sorted([(s1[p], v1_s.at[p]) for p in pieces] + [(s2[p], v2_s.at[p]) for p in pieces])
        for p in pieces:
            row = 0
            for a, nb in enumerate(_CAND_COUNTS):
                cand_s[p, row:row + nb, :] = v1_s[p, a:a + 1, :] + v2_s[p, 0:nb, :]
                row += nb
        cand = [cand_s[p] for p in pieces]
        m0 = [jnp.max(cand[p], axis=0, keepdims=True) for p in pieces]
        m = list(m0)
        z = [jnp.zeros((1, LANE), F32) for p in pieces]
        for k in range(PEER_TOPK):
            for p in pieces:
                if k:
                    m[p] = jnp.max(cand[p], axis=0, keepdims=True)
                z[p] = z[p] + jnp.exp(m[p] - m0[p])
                cand[p] = jnp.where(cand[p] == m[p], NEG_INF, cand[p])
        for p in pieces:
            h = head_of[p]
            tl = p % n_lb
            tok = slice(tl * LANE, (tl + 1) * LANE)
            tau = m[p]
            rank2 = jnp.full(s2[p].shape, float(PEER_TOPK), F32)
            for b in reversed(range(PEER_TOPK)):
                rank2 = jnp.where(s2[p] >= v2_s[p, b:b + 1, :], float(b), rank2)
            cnt = jnp.zeros(s1[p].shape, F32)
            for b in range(PEER_TOPK):
                cnt = jnp.where(s1[p] + v2_s[p, b:b + 1, :] >= tau, float(b + 1), cnt)
            l_ref[h, :, tok] = cnt
            e1_ref[h, :, tok] = jnp.exp(s1[p] - v1_s[p, 0:1, :]) * (1.0 / z[p])
            r2_ref[h, tl] = pltpu.bitcast(rank2.astype(BF16), U32)
            e2_ref[h, tl] = pltpu.bitcast(jnp.exp(s2[p] - v2_s[p, 0:1, :]).astype(BF16), U32)
        return carry

    lax.fori_loop(0, PEER_HEADS // POST_HEAD_GROUP, head, 0)


def _post(y_lru, y_rwkv, x, wo_a, wo_b, g2, wq, qg, k1, k2):
    n = x.shape[0]
    tm = POST_ROWS
    c2 = lambda i: (0, 0)
    c3 = lambda i: (0, 0, 0)
    hk = (PEER_HEADS, PEER_N_KEYS)
    tok = pl.BlockSpec(hk + (tm,), lambda i: (0, 0, i))
    tok_packed = pl.BlockSpec((PEER_HEADS, tm // LANE, PEER_N_KEYS // 2, LANE),
                              lambda i: (0, i, 0, 0))
    return pl.pallas_call(
        _post_body,
        grid=(n // tm,),
        in_specs=[
            pl.BlockSpec((tm, D_LRU), lambda i: (i, 0)),
            pl.BlockSpec((tm, D_RWKV), lambda i: (i, 0)),
            pl.BlockSpec((tm, D_MODEL), lambda i: (i, 0)),
            pl.BlockSpec((D_LRU, D_MODEL), c2),
            pl.BlockSpec((D_RWKV, D_MODEL), c2),
            pl.BlockSpec((1, D_MODEL), c2),
            pl.BlockSpec((D_MODEL, PEER_HEADS * PEER_D_QUERY), c2),
            pl.BlockSpec((1, PEER_HEADS * PEER_D_QUERY), c2),
            pl.BlockSpec(hk + (PEER_HALF,), c3),
            pl.BlockSpec(hk + (PEER_HALF,), c3),
        ],
        out_specs=[
            pl.BlockSpec((tm, D_MODEL), lambda i: (i, 0)),
            pl.BlockSpec((D_MODEL // 2, tm), lambda i: (0, i)),
            tok, tok, tok_packed, tok_packed,
        ],
        out_shape=[
            jax.ShapeDtypeStruct((n, D_MODEL), F32),
            jax.ShapeDtypeStruct((D_MODEL // 2, n), U32),
            jax.ShapeDtypeStruct(hk + (n,), F32),
            jax.ShapeDtypeStruct(hk + (n,), F32),
            jax.ShapeDtypeStruct((PEER_HEADS, n // LANE, PEER_N_KEYS // 2, LANE), U32),
            jax.ShapeDtypeStruct((PEER_HEADS, n // LANE, PEER_N_KEYS // 2, LANE), U32),
        ],
        scratch_shapes=[
            pltpu.VMEM((tm, PEER_HEADS * PEER_D_QUERY), F32),
            pltpu.VMEM((POST_HEAD_GROUP * tm // LANE, PEER_TOPK, LANE), F32),
            pltpu.VMEM((POST_HEAD_GROUP * tm // LANE, PEER_TOPK, LANE), F32),
            pltpu.VMEM((POST_HEAD_GROUP * tm // LANE, _CAND_ROWS, LANE), F32),
        ],
        compiler_params=pltpu.CompilerParams(
            dimension_semantics=("arbitrary",), vmem_limit_bytes=VMEM_LIMIT),
        name="post",
    )(y_lru, y_rwkv, x, wo_a, wo_b, g2, wq, qg, k1, k2)


def _row_bcast_bf16(rows8, ii):
    packed = jnp.broadcast_to(rows8[ii:ii + 1, :], (16, LANE)).astype(BF16)
    return jnp.tile(packed, (PEER_N_KEYS // 16, 1))


EXP_TILES = N_EXPERTS // (EXP_I * PEER_N_KEYS)
EXP_LAG = 2
EXP_MROWS = 128


def _experts_body(xt_ref, l_ref, e1_ref, r2_ref, e2_ref, u_ref, vt_ref, h1_ref, fg_ref, o_ref,
                  acc_ref, ht0_ref, ht1_ref, pt0_ref, pt1_ref, lrow_ref, erow_ref):
    s = pl.program_id(0)
    tn = EXP_TOKENS
    half = tn // 2

    @pl.when(s == 0)
    def _():
        for ref in (acc_ref, ht0_ref, ht1_ref, pt0_ref, pt1_ref):
            ref[...] = jnp.zeros_like(ref)

    e_gate = (s + EXP_TILES - 1) % EXP_TILES
    e_out = (s + EXP_TILES - 2) % EXP_TILES
    i0 = pl.multiple_of(e_gate * EXP_I, EXP_I)

    for h in range(PEER_HEADS):
        lrow_ref[h] = l_ref[h, pl.ds(i0, EXP_I), :]
        erow_ref[h] = e1_ref[h, pl.ds(i0, EXP_I), :]

    def stage(ht_w, ht_r, pt_w, pt_r):
        def hidden_job(mi, hf):
            rs = slice(mi * EXP_MROWS, (mi + 1) * EXP_MROWS)
            hl = slice(hf * half, (hf + 1) * half)
            rp = slice(mi * EXP_MROWS // 2, (mi + 1) * EXP_MROWS // 2)
            ht_w[rs, hl] = jnp.dot(pltpu.bitcast(u_ref[rp, :], BF16),
                                   pltpu.bitcast(xt_ref[:, hl], BF16), preferred_element_type=F32)

        def output_job(mi, hf):
            rs = slice(mi * EXP_MROWS, (mi + 1) * EXP_MROWS)
            hl = slice(hf * half, (hf + 1) * half)
            rp = slice(mi * EXP_MROWS // 2, (mi + 1) * EXP_MROWS // 2)
            acc_ref[rs, hl] += jnp.dot(pltpu.bitcast(vt_ref[rp, :], BF16),
                                       pltpu.bitcast(pt_r[:, hl], BF16), preferred_element_type=F32)

        def gate_piece(lb, ii):
            lanes = slice(lb * LANE, (lb + 1) * LANE)
            rows = slice(ii * PEER_N_KEYS, (ii + 1) * PEER_N_KEYS)
            rows_packed = slice(ii * PEER_N_KEYS // 2, (ii + 1) * PEER_N_KEYS // 2)
            gate = jnp.zeros((PEER_N_KEYS, LANE), BF16)
            for h in range(PEER_HEADS):
                lrow = _row_bcast_bf16(lrow_ref[h, :, lanes], ii)
                erow = _row_bcast_bf16(erow_ref[h, :, lanes], ii)
                sel = pltpu.bitcast(r2_ref[h, lb], BF16) < lrow
                val = pltpu.bitcast(e2_ref[h, lb], BF16) * erow
                gate = gate + jnp.where(sel, val, jnp.zeros_like(val))
            pt = gate * _gelu_erf(ht_r[rows, lanes]).astype(BF16)
            pt_w[rows_packed, lanes] = pltpu.bitcast(pt, U32)

        nm = (EXP_I * PEER_N_KEYS) // EXP_MROWS
        mxu_jobs = ([functools.partial(hidden_job, mi, hf) for hf in range(2) for mi in range(nm)]
                    + [functools.partial(output_job, mi, hf) for hf in range(2)
                       for mi in range(D_MODEL // EXP_MROWS)])
        pieces = [functools.partial(gate_piece, lb, ii)
                  for lb in range(tn // LANE) for ii in range(EXP_I)]
        per_job = -(-len(pieces) // len(mxu_jobs))
        for j, job in enumerate(mxu_jobs):
            job()
            for piece in pieces[j * per_job:(j + 1) * per_job]:
                piece()

    @pl.when(s % 2 == 0)
    def _():
        stage(ht0_ref, ht1_ref, pt0_ref, pt1_ref)

    @pl.when(s % 2 == 1)
    def _():
        stage(ht1_ref, ht0_ref, pt1_ref, pt0_ref)

    @pl.when(jnp.logical_and(e_out == EXP_TILES - 1, s >= EXP_LAG))
    def _():
        h2 = h1_ref[...] + acc_ref[...].T
        ms = jnp.mean(h2 * h2, axis=-1, keepdims=True)
        o_ref[...] = (h2 * lax.rsqrt(ms + NORM_EPS)) * fg_ref[...]
        acc_ref[...] = jnp.zeros_like(acc_ref)


def _experts(xt, lcnt, e1, r2, e2, u_packed, vt_packed, h1, fgain):
    n = h1.shape[0]
    tn = EXP_TOKENS
    te = EXP_I * PEER_N_KEYS
    nb = n // tn
    hk = (PEER_HEADS, PEER_N_KEYS)

    def tok_block(lag):
        return lambda s: jnp.clip((s - lag) // EXP_TILES, 0, nb - 1)

    def tile(lag):
        return lambda s: (s + EXP_TILES - lag) % EXP_TILES

    tb0, tb1, tb2 = tok_block(0), tok_block(1), tok_block(2)
    tok = pl.BlockSpec(hk + (tn,), lambda s: (0, 0, tb1(s)))
    tok_packed = pl.BlockSpec((PEER_HEADS, tn // LANE, PEER_N_KEYS // 2, LANE),
                              lambda s: (0, tb1(s), 0, 0))
    return pl.pallas_call(
        _experts_body,
        grid=(nb * EXP_TILES + EXP_LAG,),
        in_specs=[
            pl.BlockSpec((D_MODEL // 2, tn), lambda s: (0, tb0(s))),
            tok, tok, tok_packed, tok_packed,
            pl.BlockSpec((te // 2, D_MODEL), lambda s: (tile(0)(s), 0)),
            pl.BlockSpec((D_MODEL // 2, te), lambda s: (0, tile(2)(s))),
            pl.BlockSpec((tn, D_MODEL), lambda s: (tb2(s), 0)),
            pl.BlockSpec((1, D_MODEL), lambda s: (0, 0)),
        ],
        out_specs=pl.BlockSpec((tn, D_MODEL), lambda s: (tb2(s), 0)),
        out_shape=jax.ShapeDtypeStruct((n, D_MODEL), F32),
        scratch_shapes=[
            pltpu.VMEM((D_MODEL, tn), F32),
            pltpu.VMEM((te, tn), F32),
            pltpu.VMEM((te, tn), F32),
            pltpu.VMEM((te // 2, tn), U32),
            pltpu.VMEM((te // 2, tn), U32),
            pltpu.VMEM((PEER_HEADS, EXP_I, tn), F32),
            pltpu.VMEM((PEER_HEADS, EXP_I, tn), F32),
        ],
        compiler_params=pltpu.CompilerParams(
            dimension_semantics=("arbitrary",), vmem_limit_bytes=VMEM_LIMIT),
        name="experts",
    )(xt, lcnt, e1, r2, e2, u_packed, vt_packed, h1, fgain)


def _block_diag(w):
    nh, d, _ = w.shape
    eye = jnp.eye(nh, dtype=w.dtype)
    return (eye[:, None, :, None] * w[:, :, None, :]).reshape(nh * d, nh * d)


def kernel(x, meta_tokens, norm1_gain, w_in, conv_w, conv_b, lru_gate_a_w, lru_gate_a_b, lru_gate_x_w, lru_gate_x_b, lru_lambda, lru_out_gain, rwkv_shift_mu, rwkv_w0, rwkv_w2, rwkv_a0, rwkv_a2, rwkv_g2, rwkv_k_k, rwkv_k_a, rwkv_r_k, rwkv_gn_w, rwkv_gn_b, w_out, norm2_gain, peer_w_query, peer_q_gain, peer_sub_keys, peer_u, peer_v, final_norm_gain):
    bsz, seq, _ = x.shape
    row = lambda v: v.reshape(1, -1).astype(F32)

    w_in_bf = w_in[0].astype(BF16)
    w_lru, w_rwkv = w_in_bf[:, :2 * D_LRU], w_in_bf[:, 2 * D_LRU:]
    wa_bd = _block_diag(lru_gate_a_w[0]).astype(BF16)
    wx_bd = _block_diag(lru_gate_x_w[0]).astype(BF16)
    zeros_l = jnp.zeros((D_DECAY_LORA, D_RWKV), F32)
    lora_bd = jnp.concatenate(
        [jnp.concatenate([rwkv_w2[0], zeros_l], axis=1),
         jnp.concatenate([zeros_l, rwkv_a2[0]], axis=1)], axis=0).astype(BF16)
    ones_bd = _block_diag(jnp.ones((RWKV_HEADS, RWKV_HEAD_DIM, RWKV_HEAD_DIM), F32)).astype(BF16)
    tri = jnp.tril(jnp.ones((CHUNK, CHUNK), F32)).astype(BF16)
    wo_bf = w_out[0].astype(BF16)
    wq_bf = peer_w_query[0].astype(BF16)

    head = jnp.concatenate(
        [jnp.zeros((FRONT_PAD, D_MODEL), x.dtype), meta_tokens.astype(x.dtype)], axis=0)
    p_lru, p_rwkv = _inproj(head, x, row(norm1_gain[0]), w_lru, w_rwkv)
    y_lru, u_packed, vt_packed = _lru(
        p_lru, conv_w[0], row(conv_b[0]), wa_bd, row(lru_gate_a_b[0]), wx_bd,
        row(lru_gate_x_b[0]), row(lru_lambda[0]), row(lru_out_gain[0]), peer_u[0], peer_v[0], seq)
    y_rwkv = _rwkv(p_rwkv, row(rwkv_shift_mu[0]), row(rwkv_w0[0]),
                   lora_bd, row(rwkv_a0[0]), rwkv_g2[0].astype(BF16), row(rwkv_k_k[0]),
                   row(rwkv_k_a[0]), row(rwkv_r_k[0]), row(rwkv_gn_w[0]), row(rwkv_gn_b[0]),
                   ones_bd, tri, seq)

    n = bsz * seq
    h1, xt, lcnt, e1, r2, e2 = _post(
        y_lru.reshape(n, D_LRU), y_rwkv.reshape(n, D_RWKV), x.reshape(n, D_MODEL),
        wo_bf[:D_LRU], wo_bf[D_LRU:], row(norm2_gain[0]), wq_bf, row(peer_q_gain[0]),
        peer_sub_keys[0, :, 0], peer_sub_keys[0, :, 1])
    out = _experts(xt, lcnt, e1, r2, e2, u_packed, vt_packed, h1, row(final_norm_gain))
    return out.reshape(bsz, seq, D_MODEL)
```
